```python
import jax
import jax.numpy as jnp
from jax import lax
import numpy as np

D_MODEL = 2048
BATCH = 2
SEQ = 16384
DEPTH = 4

GRID_W = 64
CTX_LEN = 256

NA_HEADS = 8
NA_DIM = 128
WIN_ROWS = 8
WIN_COLS = 16
Q_COL_BLOCK = 16
KEY_COL_BAND = 32

GDN_HEADS = 8
GDN_DK = 128
GDN_DV = 128
CONV_W = 5

GLA_HEADS = 4
GLA_DK = 128
GLA_DV = 256
GLA_LOWRANK = 16
GLA_GATE_NORM = 16.0

CHUNK = 64
ROPE_BASE = 10000.0

N_BRANCH = 3
BRANCH_W = 1024

N_EXPERTS = 16
EXPERT_FF = 1024
CAPACITY_FACTOR = 2

EPS = 1e-6
F32 = jnp.float32

NA_W = NA_HEADS * NA_DIM
GDN_KW = GDN_HEADS * GDN_DK
GDN_VW = GDN_HEADS * GDN_DV
GLA_KW = GLA_HEADS * GLA_DK
GLA_VW = GLA_HEADS * GLA_DV

IN_GROUPS = (
    ('na_q', NA_W), ('na_k', NA_W), ('na_v', NA_W),
    ('gdn_q', GDN_KW), ('gdn_k', GDN_KW), ('gdn_v', GDN_VW), ('gdn_z', GDN_VW),
    ('gdn_beta', 2 * GDN_HEADS), ('gdn_a', 2 * GDN_HEADS),
    ('gla_q', GLA_KW), ('gla_k', GLA_KW), ('gla_v', GLA_VW), ('gla_z', GLA_VW),
    ('gla_lr', 2 * GLA_LOWRANK),
    ('gate_na', D_MODEL), ('gate_gdn', D_MODEL), ('gate_gla', D_MODEL),
)
IN_WIDTH = sum(width for _, width in IN_GROUPS)
GATE_GROUPS = ('gate_na', 'gate_gdn', 'gate_gla')

kernel_name = 'hybrid_natten_gdn_gla_ecmoe_dit'


def rms_norm(t, w):
    tf = t.astype(F32)
    y = tf * lax.rsqrt(jnp.mean(tf * tf, axis=-1, keepdims=True) + EPS)
    return (y * w.astype(F32)).astype(t.dtype)


def modulate(t, shift, scale):
    return t * (1.0 + scale) + shift


def l2_normalize(t):
    return t * lax.rsqrt(jnp.sum(t * t, axis=-1, keepdims=True) + EPS)


def split_heads(t, n_heads):
    return t.reshape(t.shape[:-1] + (n_heads, t.shape[-1] // n_heads))


def merge_heads(t):
    return t.reshape(t.shape[:-2] + (t.shape[-2] * t.shape[-1],))


def rev(t):
    return None if t is None else jnp.flip(t, axis=1)


def project_groups(h, w_in, names):
    out = {}
    start = 0
    for name, width in IN_GROUPS:
        if name in names:
            out[name] = h @ w_in[:, start:start + width]
        start += width
    return out


def axial_rope_angles(n_tokens, head_dim):
    axis_dim = head_dim // 2
    inv_freq = ROPE_BASE ** (-jnp.arange(0, axis_dim, 2, dtype=F32) / axis_dim)
    pos = jnp.arange(n_tokens)
    row = (pos // GRID_W).astype(F32)
    col = (pos % GRID_W).astype(F32)
    return row[:, None] * inv_freq, col[:, None] * inv_freq


def rope_rotate(t, ang):
    cos = jnp.cos(ang)[:, None, :]
    sin = jnp.sin(ang)[:, None, :]
    t1, t2 = jnp.split(t, 2, axis=-1)
    return jnp.concatenate([t1 * cos - t2 * sin, t2 * cos + t1 * sin], axis=-1)


def apply_rope2d(t, ang):
    ang_row, ang_col = ang
    tf = t.astype(F32)
    half = tf.shape[-1] // 2
    return jnp.concatenate([rope_rotate(tf[..., :half], ang_row), rope_rotate(tf[..., half:], ang_col)], axis=-1)


def centred_conv(t, w):
    pad = CONV_W // 2
    return lax.conv_general_dilated(
        t, w[:, None, :].astype(t.dtype), window_strides=(1,), padding=((pad, pad),),
        dimension_numbers=('NWC', 'WIO', 'NWC'), feature_group_count=t.shape[-1])


def gated_head_norm(o, z, w):
    y = o * lax.rsqrt(jnp.mean(o * o, axis=-1, keepdims=True) + EPS) * w.astype(F32)
    return (merge_heads(y) * jax.nn.silu(z.astype(F32))).astype(z.dtype)


def neighbourhood_attention(q, k, v, k_ctx, v_ctx, rpb, rows):
    bsz, n_tok, n_heads, dh = q.shape
    kr = min(WIN_ROWS, rows)
    n_cb = GRID_W // Q_COL_BLOCK
    scale = dh ** -0.5
    q = q.reshape(bsz, rows, n_cb, Q_COL_BLOCK, n_heads, dh)
    k = k.reshape(bsz, rows, GRID_W, n_heads, dh)
    v = v.reshape(bsz, rows, GRID_W, n_heads, dh)
    qcol = jnp.arange(GRID_W).reshape(n_cb, Q_COL_BLOCK)
    band = jnp.clip(qcol[:, :1] - WIN_COLS // 2, 0, GRID_W - KEY_COL_BAND) + jnp.arange(KEY_COL_BAND)
    win_start = jnp.clip(qcol - WIN_COLS // 2, 0, GRID_W - WIN_COLS)
    col_ok = (band[:, None, :] >= win_start[..., None]) & (band[:, None, :] < win_start[..., None] + WIN_COLS)
    dc_idx = jnp.clip(band[:, None, :] - qcol[..., None] + WIN_COLS - 1, 0, 2 * WIN_COLS - 2)
    rpb_f = rpb.astype(F32)

    def one_row(r):
        rs = jnp.clip(r - kr // 2, 0, rows - kr)
        k_rows = lax.dynamic_slice_in_dim(k, rs, kr, axis=1)
        v_rows = lax.dynamic_slice_in_dim(v, rs, kr, axis=1)
        k_band = k_rows[:, :, band]
        v_band = v_rows[:, :, band]
        q_r = lax.dynamic_index_in_dim(q, r, axis=1, keepdims=False)
        s_win = jnp.einsum('bnqhd,brnkhd->bhnqrk', q_r, k_band, preferred_element_type=F32) * scale
        dr_idx = rs + jnp.arange(kr) - r + WIN_ROWS - 1
        bias = rpb_f[:, dr_idx][:, :, dc_idx].transpose(0, 2, 3, 1, 4)
        s_win = jnp.where(col_ok[:, :, None, :], s_win + bias, -jnp.inf)
        s_win = s_win.reshape(bsz, n_heads, n_cb, Q_COL_BLOCK, kr * KEY_COL_BAND)
        s_ctx = jnp.einsum('bnqhd,blhd->bhnql', q_r, k_ctx, preferred_element_type=F32) * scale
        p = jax.nn.softmax(jnp.concatenate([s_win, s_ctx], axis=-1), axis=-1).astype(v.dtype)
        p_win = p[..., :kr * KEY_COL_BAND].reshape(bsz, n_heads, n_cb, Q_COL_BLOCK, kr, KEY_COL_BAND)
        p_ctx = p[..., kr * KEY_COL_BAND:]
        return (jnp.einsum('bhnqrk,brnkhd->bnqhd', p_win, v_band)
                + jnp.einsum('bhnql,blhd->bnqhd', p_ctx, v_ctx))

    out = lax.map(one_row, jnp.arange(rows))
    return jnp.moveaxis(out, 0, 1).reshape(bsz, n_tok, n_heads, dh)


def context_attention(q, k, v):
    s = jnp.einsum('blhd,bmhd->bhlm', q, k, preferred_element_type=F32) * q.shape[-1] ** -0.5
    p = jax.nn.softmax(s, axis=-1).astype(v.dtype)
    return jnp.einsum('bhlm,bmhd->blhd', p, v)


def na_mixer(h, hc, w_in, rpb, rows, ctx_out):
    names = ('na_q', 'na_k', 'na_v')
    pl = project_groups(h, w_in, names)
    pc = project_groups(hc, w_in, names if ctx_out else ('na_k', 'na_v'))
    k_c = split_heads(pc['na_k'], NA_HEADS)
    v_c = split_heads(pc['na_v'], NA_HEADS)
    y = neighbourhood_attention(split_heads(pl['na_q'], NA_HEADS), split_heads(pl['na_k'], NA_HEADS),
                                split_heads(pl['na_v'], NA_HEADS), k_c, v_c, rpb, rows)
    y = merge_heads(y)
    yc = merge_heads(context_attention(split_heads(pc['na_q'], NA_HEADS), k_c, v_c)) if ctx_out else None
    return y, yc


def bidirectional_with_prefix(scan_fn, lat_fwd, lat_bwd, ctx_fwd, ctx_bwd, s0):
    oc_f, sc_f = scan_fn(*ctx_fwd, s0)
    ol_f, _ = scan_fn(*lat_fwd, sc_f)
    oc_b, sc_b = scan_fn(*[rev(t) for t in ctx_bwd], s0)
    ol_b, _ = scan_fn(*[rev(t) for t in lat_bwd], sc_b)
    o_lat = ol_f + rev(ol_b)
    o_ctx = None if oc_f is None else oc_f + rev(oc_b)
    return o_lat, o_ctx


def chunk_gated_delta(q, k, v, g, beta, s0):
    bsz, n_tok, n_heads, _ = k.shape
    n = n_tok // CHUNK
    to_chunks = lambda t: t.reshape(bsz, n, CHUNK, n_heads, t.shape[-1]).transpose(1, 0, 3, 2, 4)
    kc, vc = to_chunks(k), to_chunks(v)
    gc = jnp.cumsum(g.reshape(bsz, n, CHUNK, n_heads).transpose(1, 0, 3, 2), axis=-1)
    bc = beta.reshape(bsz, n, CHUNK, n_heads).transpose(1, 0, 3, 2)
    causal = jnp.tril(jnp.ones((CHUNK, CHUNK), bool))
    strict = jnp.tril(jnp.ones((CHUNK, CHUNK), bool), -1)
    decay = jnp.exp(jnp.where(causal, gc[..., :, None] - gc[..., None, :], -jnp.inf))
    kb = kc * bc[..., None]
    lower = jnp.where(strict, jnp.einsum('nbhik,nbhjk->nbhij', kb, kc) * decay, 0.0)
    unit_lower = lower + jnp.eye(CHUNK, dtype=lower.dtype)
    u = lax.linalg.triangular_solve(unit_lower, vc * bc[..., None], left_side=True, lower=True, unit_diagonal=True)
    w = lax.linalg.triangular_solve(unit_lower, kb * jnp.exp(gc)[..., None], left_side=True, lower=True,
                                    unit_diagonal=True)
    g_last = gc[..., -1]
    k_dec = kc * jnp.exp(g_last[..., None] - gc)[..., None]

    def new_values(s, u_i, w_i):
        return u_i - jnp.einsum('bhck,bhkv->bhcv', w_i, s)

    def update(s, v_new, kd_i, gl_i):
        return s * jnp.exp(gl_i)[..., None, None] + jnp.einsum('bhck,bhcv->bhkv', kd_i, v_new)

    if q is None:
        def state_step(s, xs):
            u_i, w_i, kd_i, gl_i = xs
            return update(s, new_values(s, u_i, w_i), kd_i, gl_i), None
        s_fin, _ = lax.scan(state_step, s0, (u, w, k_dec, g_last))
        return None, s_fin

    qc = to_chunks(q)
    q_dec = qc * jnp.exp(gc)[..., None]
    qk = jnp.where(causal, jnp.einsum('nbhik,nbhjk->nbhij', qc, kc) * decay, 0.0)

    def step(s, xs):
        u_i, w_i, kd_i, gl_i, qd_i, qk_i = xs
        v_new = new_values(s, u_i, w_i)
        o = jnp.einsum('bhck,bhkv->bhcv', qd_i, s) + jnp.einsum('bhij,bhjv->bhiv', qk_i, v_new)
        return update(s, v_new, kd_i, gl_i), o

    s_fin, o = lax.scan(step, s0, (u, w, k_dec, g_last, q_dec, qk))
    return o.transpose(1, 0, 3, 2, 4).reshape(bsz, n_tok, n_heads, -1), s_fin


def chunk_gla(q, k, v, log_a, s0):
    bsz, n_tok, n_heads, _ = k.shape
    n = n_tok // CHUNK
    to_chunks = lambda t: t.reshape(bsz, n, CHUNK, n_heads, t.shape[-1]).transpose(1, 0, 3, 2, 4)
    kc, vc = to_chunks(k), to_chunks(v)
    b = jnp.cumsum(to_chunks(log_a), axis=-2)
    b_last = b[..., -1:, :]
    k_dec = kc * jnp.exp(b_last - b)
    d_last = jnp.exp(b_last[..., 0, :])

    def update(s, k_i, v_i, d_i):
        return d_i[..., None] * s + jnp.einsum('bhck,bhcv->bhkv', k_i, v_i)

    if q is None:
        s_fin, _ = lax.scan(lambda s, xs: (update(s, *xs), None), s0, (k_dec, vc, d_last))
        return None, s_fin

    q_dec = to_chunks(q) * jnp.exp(b)
    k_inv = kc * jnp.exp(-b)
    causal = jnp.tril(jnp.ones((CHUNK, CHUNK), bool))
    scores = jnp.where(causal, jnp.einsum('nbhik,nbhjk->nbhij', q_dec, k_inv), 0.0)
    o_intra = jnp.einsum('nbhij,nbhjv->nbhiv', scores, vc)

    def step(s, xs):
        q_i, k_i, v_i, d_i = xs
        return update(s, k_i, v_i, d_i), jnp.einsum('bhck,bhkv->bhcv', q_i, s)

    s_fin, o_inter = lax.scan(step, s0, (q_dec, k_dec, vc, d_last))
    o = (o_intra + o_inter).transpose(1, 0, 3, 2, 4).reshape(bsz, n_tok, n_heads, -1)
    return o, s_fin


def gdn_prepare(pg, conv_w, a_log, dt_bias):
    bsz, n_tok, _ = pg['gdn_k'].shape
    conv_q, conv_k, conv_v = conv_w[:, :GDN_KW], conv_w[:, GDN_KW:2 * GDN_KW], conv_w[:, 2 * GDN_KW:]

    def conv_act(t, w):
        return split_heads(jax.nn.silu(centred_conv(t, w)), GDN_HEADS).astype(F32)

    k = l2_normalize(conv_act(pg['gdn_k'], conv_k))
    v = conv_act(pg['gdn_v'], conv_v)
    q = l2_normalize(conv_act(pg['gdn_q'], conv_q)) * GDN_DK ** -0.5 if 'gdn_q' in pg else None
    beta = jax.nn.sigmoid(pg['gdn_beta'].astype(F32)).reshape(bsz, n_tok, 2, GDN_HEADS)
    g = -jnp.exp(a_log.astype(F32)) * jax.nn.softplus(
        pg['gdn_a'].astype(F32).reshape(bsz, n_tok, 2, GDN_HEADS) + dt_bias.astype(F32))
    return q, k, v, beta, g


def gdn_mixer(h, hc, w_in, conv_w, a_log, dt_bias, norm_w, ctx_out):
    lat_names = ('gdn_q', 'gdn_k', 'gdn_v', 'gdn_z', 'gdn_beta', 'gdn_a')
    ctx_names = lat_names if ctx_out else ('gdn_k', 'gdn_v', 'gdn_beta', 'gdn_a')
    pl = project_groups(h, w_in, lat_names)
    pc = project_groups(hc, w_in, ctx_names)
    ql, kl, vl, bl, gl = gdn_prepare(pl, conv_w, a_log, dt_bias)
    qc, kc, vc, bc, gc = gdn_prepare(pc, conv_w, a_log, dt_bias)
    s0 = jnp.zeros((h.shape[0], GDN_HEADS, GDN_DK, GDN_DV), F32)
    o_lat, o_ctx = bidirectional_with_prefix(
        chunk_gated_delta,
        (ql, kl, vl, gl[:, :, 0], bl[:, :, 0]), (ql, kl, vl, gl[:, :, 1], bl[:, :, 1]),
        (qc, kc, vc, gc[:, :, 0], bc[:, :, 0]), (qc, kc, vc, gc[:, :, 1], bc[:, :, 1]), s0)
    y = gated_head_norm(o_lat, pl['gdn_z'], norm_w)
    yc = gated_head_norm(o_ctx, pc['gdn_z'], norm_w) if ctx_out else None
    return y, yc


def gla_prepare(pg, w2, b2, ang):
    bsz, n_tok, _ = pg['gla_k'].shape

    def qk_heads(t):
        t = split_heads(t, GLA_HEADS)
        if ang is not None:
            t = apply_rope2d(t, ang)
        return t.astype(F32)

    k = qk_heads(pg['gla_k'])
    q = qk_heads(pg['gla_q']) * GLA_DK ** -0.5 if 'gla_q' in pg else None
    v = split_heads(pg['gla_v'], GLA_HEADS).astype(F32)
    lr = pg['gla_lr'].astype(F32).reshape(bsz, n_tok, 2, GLA_LOWRANK)
    log_a = jax.nn.log_sigmoid(jnp.einsum('btzr,zrk->btzk', lr, w2.astype(F32)) + b2.astype(F32)) / GLA_GATE_NORM
    return q, k, v, log_a.reshape(bsz, n_tok, 2, GLA_HEADS, GLA_DK)


def gla_mixer(h, hc, w_in, w2, b2, norm_w, ang, ctx_out):
    lat_names = ('gla_q', 'gla_k', 'gla_v', 'gla_z', 'gla_lr')
    ctx_names = lat_names if ctx_out else ('gla_k', 'gla_v', 'gla_lr')
    pl = project_groups(h, w_in, lat_names)
    pc = project_groups(hc, w_in, ctx_names)
    ql, kl, vl, al = gla_prepare(pl, w2, b2, ang)
    qc, kc, vc, ac = gla_prepare(pc, w2, b2, None)
    s0 = jnp.zeros((h.shape[0], GLA_HEADS, GLA_DK, GLA_DV), F32)
    o_lat, o_ctx = bidirectional_with_prefix(
        chunk_gla,
        (ql, kl, vl, al[:, :, 0]), (ql, kl, vl, al[:, :, 1]),
        (qc, kc, vc, ac[:, :, 0]), (qc, kc, vc, ac[:, :, 1]), s0)
    y = gated_head_norm(o_lat, pl['gla_z'], norm_w)
    yc = gated_head_norm(o_ctx, pc['gla_z'], norm_w) if ctx_out else None
    return y, yc


def merge_branches(h, w_in, branches, w_branch, w_out):
    gates = project_groups(h, w_in, GATE_GROUPS)
    y = sum(jax.nn.sigmoid(gates[name]) * (br @ w_branch[i])
            for i, (name, br) in enumerate(zip(GATE_GROUPS, branches)))
    return y @ w_out


def expert_choice_ffn(h, w_router, w_gate, w_up, w_down):
    bsz, n_tok, d = h.shape
    cap = max(1, CAPACITY_FACTOR * n_tok // N_EXPERTS)
    logits = jnp.einsum('btd,de->bte', h, w_router, preferred_element_type=F32)
    affinity = jax.nn.softmax(logits, axis=-1)
    gate, idx = lax.top_k(jnp.swapaxes(affinity, 1, 2), cap)
    xs = jax.vmap(lambda hb, ib: hb[ib])(h, idx)
    hid = jax.nn.silu(jnp.einsum('becd,edf->becf', xs, w_gate)) * jnp.einsum('becd,edf->becf', xs, w_up)
    ys = jnp.einsum('becf,efd->becd', hid, w_down) * gate[..., None].astype(h.dtype)
    return jax.vmap(lambda yb, ib: jnp.zeros((n_tok, d), h.dtype).at[ib.reshape(-1)].add(yb.reshape(-1, d)))(ys, idx)


def setup_inputs(seed: int = 0) -> dict:
    key = jax.random.key(seed)
    ks = jax.random.split(key, 24)
    d = D_MODEL

    def nrm(k, shape, s):
        return jax.random.normal(k, shape, F32) * s

    dt = jnp.exp(jax.random.uniform(ks[13], (DEPTH, 2, GDN_HEADS), F32, jnp.log(1e-3), jnp.log(1e-1)))
    return {
        'x': nrm(ks[0], (BATCH, SEQ, d), 1.0),
        'c': nrm(ks[1], (BATCH, d), 1.0),
        'ctx': nrm(ks[2], (BATCH, CTX_LEN, d), 1.0),
        'c_ctx': nrm(ks[3], (d,), 1.0),
        'norm1_w': 1.0 + nrm(ks[4], (DEPTH, d), 0.05),
        'norm2_w': 1.0 + nrm(ks[5], (DEPTH, d), 0.05),
        'final_norm_w': 1.0 + nrm(ks[6], (d,), 0.05),
        'w_ada': nrm(ks[7], (DEPTH, d, 6 * d), 0.5 * d ** -0.5),
        'b_ada': nrm(ks[8], (DEPTH, 6 * d), 0.02),
        'w_in': nrm(ks[9], (DEPTH, d, IN_WIDTH), d ** -0.5),
        'na_rpb': nrm(ks[10], (DEPTH, NA_HEADS, 2 * WIN_ROWS - 1, 2 * WIN_COLS - 1), 0.1),
        'gdn_conv': nrm(ks[11], (DEPTH, CONV_W, 2 * GDN_KW + GDN_VW), CONV_W ** -0.5),
        'gdn_a_log': jnp.log(jax.random.uniform(ks[12], (DEPTH, 2, GDN_HEADS), F32, 1.0, 16.0)),
        'gdn_dt_bias': dt + jnp.log(-jnp.expm1(-dt)),
        'gdn_norm_w': 1.0 + nrm(ks[14], (DEPTH, GDN_DV), 0.05),
        'gla_w2': nrm(ks[15], (DEPTH, 2, GLA_LOWRANK, GLA_KW), GLA_LOWRANK ** -0.5),
        'gla_b2': nrm(ks[16], (DEPTH, 2, GLA_KW), 0.1),
        'gla_norm_w': 1.0 + nrm(ks[17], (DEPTH, GLA_DV), 0.05),
        'w_branch': nrm(ks[18], (DEPTH, N_BRANCH, BRANCH_W, d), BRANCH_W ** -0.5),
        'w_out': nrm(ks[19], (DEPTH, d, d), d ** -0.5),
        'w_router': nrm(ks[20], (DEPTH, d, N_EXPERTS), d ** -0.5),
        'w_gate': nrm(ks[21], (DEPTH, N_EXPERTS, d, EXPERT_FF), d ** -0.5),
        'w_up': nrm(ks[22], (DEPTH, N_EXPERTS, d, EXPERT_FF), d ** -0.5),
        'w_down': nrm(ks[23], (DEPTH, N_EXPERTS, EXPERT_FF, d), EXPERT_FF ** -0.5),
    }


def reference(x, c, ctx, c_ctx, norm1_w, norm2_w, final_norm_w, w_ada, b_ada, w_in, na_rpb, gdn_conv,
              gdn_a_log, gdn_dt_bias, gdn_norm_w, gla_w2, gla_b2, gla_norm_w, w_branch, w_out,
              w_router, w_gate, w_up, w_down):
    n_lat = x.shape[1]
    rows = n_lat // GRID_W
    ang = axial_rope_angles(n_lat, GLA_DK)
    cond_lat = jax.nn.silu(c)
    cond_ctx = jax.nn.silu(c_ctx)[None]
    for layer in range(DEPTH):
        ctx_out = layer < DEPTH - 1
        mod_lat = jnp.split((cond_lat @ w_ada[layer] + b_ada[layer])[:, None, :], 6, axis=-1)
        mod_ctx = jnp.split((cond_ctx @ w_ada[layer] + b_ada[layer])[:, None, :], 6, axis=-1)
        h = modulate(rms_norm(x, norm1_w[layer]), mod_lat[0], mod_lat[1])
        hc = modulate(rms_norm(ctx, norm1_w[layer]), mod_ctx[0], mod_ctx[1])
        ya, yca = na_mixer(h, hc, w_in[layer], na_rpb[layer], rows, ctx_out)
        yb, ycb = gdn_mixer(h, hc, w_in[layer], gdn_conv[layer], gdn_a_log[layer], gdn_dt_bias[layer],
                            gdn_norm_w[layer], ctx_out)
        yg, ycg = gla_mixer(h, hc, w_in[layer], gla_w2[layer], gla_b2[layer], gla_norm_w[layer], ang, ctx_out)
        x = x + mod_lat[2] * merge_branches(h, w_in[layer], (ya, yb, yg), w_branch[layer], w_out[layer])
        h2 = modulate(rms_norm(x, norm2_w[layer]), mod_lat[3], mod_lat[4])
        x = x + mod_lat[5] * expert_choice_ffn(h2, w_router[layer], w_gate[layer], w_up[layer], w_down[layer])
        if ctx_out:
            ctx = ctx + mod_ctx[2] * merge_branches(hc, w_in[layer], (yca, ycb, ycg), w_branch[layer], w_out[layer])
            hc2 = modulate(rms_norm(ctx, norm2_w[layer]), mod_ctx[3], mod_ctx[4])
            ctx = ctx + mod_ctx[5] * expert_choice_ffn(hc2, w_router[layer], w_gate[layer], w_up[layer],
                                                       w_down[layer])
    return rms_norm(x, final_norm_w)
```

```python
import functools

import jax
import jax.numpy as jnp
from jax import lax
from jax.experimental import pallas as pl
from jax.experimental.pallas import tpu as pltpu

D_MODEL = 2048
DEPTH = 4
GRID_W = 64

NA_HEADS = 8
NA_DIM = 128
WIN_ROWS = 8
WIN_COLS = 16
Q_COL_BLOCK = 16
KEY_COL_BAND = 32

GDN_HEADS = 8
GDN_DK = 128
GDN_DV = 128
CONV_W = 5

GLA_HEADS = 4
GLA_DK = 128
GLA_DV = 256
GLA_LOWRANK = 16
GLA_GATE_NORM = 16.0

CHUNK = 64
ROPE_BASE = 10000.0

N_BRANCH = 3
BRANCH_W = 1024

N_EXPERTS = 16
EXPERT_FF = 1024
CAPACITY_FACTOR = 2

EPS = 1e-6
F32 = jnp.float32
BF16 = jnp.bfloat16

NA_W = NA_HEADS * NA_DIM
GDN_KW = GDN_HEADS * GDN_DK
GDN_VW = GDN_HEADS * GDN_DV
GLA_KW = GLA_HEADS * GLA_DK
GLA_VW = GLA_HEADS * GLA_DV

IN_GROUPS = (
    ('na_q', NA_W), ('na_k', NA_W), ('na_v', NA_W),
    ('gdn_q', GDN_KW), ('gdn_k', GDN_KW), ('gdn_v', GDN_VW), ('gdn_z', GDN_VW),
    ('gdn_beta', 2 * GDN_HEADS), ('gdn_a', 2 * GDN_HEADS),
    ('gla_q', GLA_KW), ('gla_k', GLA_KW), ('gla_v', GLA_VW), ('gla_z', GLA_VW),
    ('gla_lr', 2 * GLA_LOWRANK),
    ('gate_na', D_MODEL), ('gate_gdn', D_MODEL), ('gate_gla', D_MODEL),
)
IN_OFFSET = {}
_off = 0
for _name, _width in IN_GROUPS:
    IN_OFFSET[_name] = (_off, _width)
    _off += _width

WIDE_GROUPS = ('na_q', 'na_k', 'na_v', 'gdn_q', 'gdn_k', 'gdn_v', 'gdn_z',
               'gla_q', 'gla_k', 'gla_v', 'gla_z', 'gate_na', 'gate_gdn', 'gate_gla')
NARROW_GROUPS = ('gdn_beta', 'gdn_a', 'gla_lr')
WIDE_OFFSET = {}
_off = 0
for _name in WIDE_GROUPS:
    WIDE_OFFSET[_name] = (_off, IN_OFFSET[_name][1])
    _off += IN_OFFSET[_name][1]
WIDE_WIDTH = _off
NARROW_OFFSET = {}
_off = 0
for _name in NARROW_GROUPS:
    NARROW_OFFSET[_name] = (_off, IN_OFFSET[_name][1])
    _off += IN_OFFSET[_name][1]
LANES = 128
NARROW_WIDTH = LANES

VMEM_LIMIT = 56 * 1024 * 1024


def _cparams(*sem):
    return pltpu.CompilerParams(dimension_semantics=sem, vmem_limit_bytes=VMEM_LIMIT)


def _norm_mod_rows(x, nw, shift, scale):
    y = x * lax.rsqrt(jnp.mean(x * x, axis=-1, keepdims=True) + EPS) * nw
    return y * (1.0 + scale) + shift


def _proj_kernel(x_ref, nw_ref, sh_ref, sc_ref, w_ref, wn_ref, o_ref, on_ref, h_scr, *, row_chunk):
    n = pl.program_id(2)
    tm = x_ref.shape[1]

    @pl.when(n == 0)
    def _():
        def body(i, carry):
            rows = pl.ds(pl.multiple_of(i * row_chunk, row_chunk), row_chunk)
            h = _norm_mod_rows(x_ref[0, rows, :], nw_ref[...], sh_ref[0], sc_ref[0])
            h_scr[rows, :] = h.astype(BF16)
            return carry
        lax.fori_loop(0, tm // row_chunk, body, 0)
        on_ref[0] = jnp.dot(h_scr[...], wn_ref[...], preferred_element_type=F32)

    o_ref[0] = jnp.dot(h_scr[...], w_ref[...], preferred_element_type=F32).astype(o_ref.dtype)


def norm_mod_project(x, nw, shift, scale, w_wide, w_narrow, *, tm, tn):
    bsz, n_tok, d = x.shape
    n_wide = w_wide.shape[1]
    n_narrow = w_narrow.shape[1]
    tm = min(tm, n_tok)
    row_chunk = min(128, tm)
    grid = (bsz, n_tok // tm, n_wide // tn)
    return pl.pallas_call(
        functools.partial(_proj_kernel, row_chunk=row_chunk),
        grid=grid,
        in_specs=[
            pl.BlockSpec((1, tm, d), lambda b, m, n: (b, m, 0)),
            pl.BlockSpec((1, d), lambda b, m, n: (0, 0)),
            pl.BlockSpec((1, 1, d), lambda b, m, n: (b, 0, 0)),
            pl.BlockSpec((1, 1, d), lambda b, m, n: (b, 0, 0)),
            pl.BlockSpec((d, tn), lambda b, m, n: (0, n)),
            pl.BlockSpec((d, n_narrow), lambda b, m, n: (0, 0)),
        ],
        out_specs=[
            pl.BlockSpec((1, tm, tn), lambda b, m, n: (b, m, n)),
            pl.BlockSpec((1, tm, n_narrow), lambda b, m, n: (b, m, 0)),
        ],
        out_shape=[
            jax.ShapeDtypeStruct((bsz, n_tok, n_wide), BF16),
            jax.ShapeDtypeStruct((bsz, n_tok, n_narrow), F32),
        ],
        scratch_shapes=[pltpu.VMEM((tm, d), BF16)],
        compiler_params=_cparams("parallel", "parallel", "arbitrary"),
        name="norm_mod_project",
    )(x, nw.reshape(1, d), shift, scale, w_wide, w_narrow)


def _merge_kernel(ga_ref, gb_ref, gg_ref, ya_ref, yb_ref, yg_ref, wb_ref, o_ref):
    acc = jax.nn.sigmoid(ga_ref[0].astype(F32)) * jnp.dot(ya_ref[0], wb_ref[0], preferred_element_type=F32)
    acc += jax.nn.sigmoid(gb_ref[0].astype(F32)) * jnp.dot(yb_ref[0], wb_ref[1], preferred_element_type=F32)
    acc += jax.nn.sigmoid(gg_ref[0].astype(F32)) * jnp.dot(yg_ref[0], wb_ref[2], preferred_element_type=F32)
    o_ref[0] = acc.astype(o_ref.dtype)


def merge_branches(proj, ya, yb, yg, w_branch, *, tm, tn):
    bsz, n_tok, _ = proj.shape
    d = w_branch.shape[2]
    tm = min(tm, n_tok)
    gate_blk = [WIDE_OFFSET[name][0] // tn for name in ('gate_na', 'gate_gdn', 'gate_gla')]

    def gate_spec(blk):
        return pl.BlockSpec((1, tm, tn), lambda b, m, n: (b, m, blk + n))

    br_spec = pl.BlockSpec((1, tm, BRANCH_W), lambda b, m, n: (b, m, 0))
    return pl.pallas_call(
        _merge_kernel,
        grid=(bsz, n_tok // tm, d // tn),
        in_specs=[gate_spec(gate_blk[0]), gate_spec(gate_blk[1]), gate_spec(gate_blk[2]),
                  br_spec, br_spec, br_spec,
                  pl.BlockSpec((N_BRANCH, BRANCH_W, tn), lambda b, m, n: (0, 0, n))],
        out_specs=pl.BlockSpec((1, tm, tn), lambda b, m, n: (b, m, n)),
        out_shape=jax.ShapeDtypeStruct((bsz, n_tok, d), BF16),
        compiler_params=_cparams("parallel", "parallel", "arbitrary"),
        name="merge_branches",
    )(proj, proj, proj, ya, yb, yg, w_branch)


def _out_residual_kernel(y_ref, w_ref, x_ref, g_ref, o_ref):
    o_ref[0] = x_ref[0] + g_ref[0] * jnp.dot(y_ref[0], w_ref[...], preferred_element_type=F32)


def out_proj_residual(y, w_out, x, gate, *, tm, tn):
    bsz, n_tok, d = x.shape
    tm = min(tm, n_tok)
    return pl.pallas_call(
        _out_residual_kernel,
        grid=(bsz, n_tok // tm, d // tn),
        in_specs=[
            pl.BlockSpec((1, tm, d), lambda b, m, n: (b, m, 0)),
            pl.BlockSpec((d, tn), lambda b, m, n: (0, n)),
            pl.BlockSpec((1, tm, tn), lambda b, m, n: (b, m, n)),
            pl.BlockSpec((1, 1, tn), lambda b, m, n: (b, 0, n)),
        ],
        out_specs=pl.BlockSpec((1, tm, tn), lambda b, m, n: (b, m, n)),
        out_shape=jax.ShapeDtypeStruct((bsz, n_tok, d), F32),
        compiler_params=_cparams("parallel", "parallel", "arbitrary"),
        name="out_proj_residual",
    )(y, w_out, x, gate)


def _norm_router_kernel(x_ref, nw_ref, sh_ref, sc_ref, wr_ref, h_ref, l_ref):
    h = _norm_mod_rows(x_ref[0], nw_ref[...], sh_ref[0], sc_ref[0])
    h_ref[0] = h.astype(h_ref.dtype)
    l_ref[0] = jnp.dot(h, wr_ref[...], preferred_element_type=F32, precision=lax.Precision.HIGHEST)


def norm_mod_router(x, nw, shift, scale, w_router_pad, *, tm):
    bsz, n_tok, d = x.shape
    n_pad = w_router_pad.shape[1]
    tm = min(tm, n_tok)
    return pl.pallas_call(
        _norm_router_kernel,
        grid=(bsz, n_tok // tm),
        in_specs=[
            pl.BlockSpec((1, tm, d), lambda b, m: (b, m, 0)),
            pl.BlockSpec((1, d), lambda b, m: (0, 0)),
            pl.BlockSpec((1, 1, d), lambda b, m: (b, 0, 0)),
            pl.BlockSpec((1, 1, d), lambda b, m: (b, 0, 0)),
            pl.BlockSpec((d, n_pad), lambda b, m: (0, 0)),
        ],
        out_specs=[
            pl.BlockSpec((1, tm, d), lambda b, m: (b, m, 0)),
            pl.BlockSpec((1, tm, n_pad), lambda b, m: (b, m, 0)),
        ],
        out_shape=[
            jax.ShapeDtypeStruct((bsz, n_tok, d), BF16),
            jax.ShapeDtypeStruct((bsz, n_tok, n_pad), F32),
        ],
        compiler_params=_cparams("parallel", "parallel"),
        name="norm_mod_router",
    )(x, nw.reshape(1, d), shift, scale, w_router_pad)


def _expert_ffn_kernel(xs_ref, g_ref, wg_ref, wu_ref, wd_ref, o_ref):
    xs = xs_ref[0, 0]
    hid = jax.nn.silu(jnp.dot(xs, wg_ref[0], preferred_element_type=F32)) * jnp.dot(
        xs, wu_ref[0], preferred_element_type=F32)
    ys = jnp.dot(hid.astype(BF16), wd_ref[0], preferred_element_type=F32)
    o_ref[0, 0] = ys * g_ref[0, 0]


def expert_ffn(xs, gate, w_gate, w_up, w_down, *, tc):
    bsz, n_exp, cap, d = xs.shape
    ff = w_gate.shape[2]
    tc = min(tc, cap)
    return pl.pallas_call(
        _expert_ffn_kernel,
        grid=(n_exp, bsz, cap // tc),
        in_specs=[
            pl.BlockSpec((1, 1, tc, d), lambda e, b, c: (b, e, c, 0)),
            pl.BlockSpec((1, 1, tc, 1), lambda e, b, c: (b, e, c, 0)),
            pl.BlockSpec((1, d, ff), lambda e, b, c: (e, 0, 0)),
            pl.BlockSpec((1, d, ff), lambda e, b, c: (e, 0, 0)),
            pl.BlockSpec((1, ff, d), lambda e, b, c: (e, 0, 0)),
        ],
        out_specs=pl.BlockSpec((1, 1, tc, d), lambda e, b, c: (b, e, c, 0)),
        out_shape=jax.ShapeDtypeStruct((bsz, n_exp, cap, d), F32),
        compiler_params=_cparams("parallel", "parallel", "arbitrary"),
        name="expert_ffn",
    )(xs, gate, w_gate, w_up, w_down)


def _final_norm_kernel(x_ref, w_ref, o_ref):
    x = x_ref[0]
    o_ref[0] = x * lax.rsqrt(jnp.mean(x * x, axis=-1, keepdims=True) + EPS) * w_ref[...]


def final_norm(x, w, *, tm):
    bsz, n_tok, d = x.shape
    return pl.pallas_call(
        _final_norm_kernel,
        grid=(bsz, n_tok // tm),
        in_specs=[pl.BlockSpec((1, tm, d), lambda b, m: (b, m, 0)),
                  pl.BlockSpec((1, d), lambda b, m: (0, 0))],
        out_specs=pl.BlockSpec((1, tm, d), lambda b, m: (b, m, 0)),
        out_shape=jax.ShapeDtypeStruct((bsz, n_tok, d), F32),
        compiler_params=_cparams("parallel", "parallel"),
        name="final_norm",
    )(x, w.reshape(1, d))


def split_heads(t, n_heads):
    return t.reshape(t.shape[:-1] + (n_heads, t.shape[-1] // n_heads))


def merge_heads(t):
    return t.reshape(t.shape[:-2] + (t.shape[-2] * t.shape[-1],))


def rev(t):
    return None if t is None else jnp.flip(t, axis=1)


def l2_normalize(t):
    return t * lax.rsqrt(jnp.sum(t * t, axis=-1, keepdims=True) + EPS)


def axial_rope_angles(n_tokens, head_dim):
    axis_dim = head_dim // 2
    inv_freq = ROPE_BASE ** (-jnp.arange(0, axis_dim, 2, dtype=F32) / axis_dim)
    pos = jnp.arange(n_tokens)
    row = (pos // GRID_W).astype(F32)
    col = (pos % GRID_W).astype(F32)
    return row[:, None] * inv_freq, col[:, None] * inv_freq


def rope_rotate(t, ang):
    cos = jnp.cos(ang)[:, None, :]
    sin = jnp.sin(ang)[:, None, :]
    t1, t2 = jnp.split(t, 2, axis=-1)
    return jnp.concatenate([t1 * cos - t2 * sin, t2 * cos + t1 * sin], axis=-1)


def apply_rope2d(t, ang):
    ang_row, ang_col = ang
    tf = t.astype(F32)
    half = tf.shape[-1] // 2
    return jnp.concatenate([rope_rotate(tf[..., :half], ang_row), rope_rotate(tf[..., half:], ang_col)], axis=-1)


def centred_conv(t, w):
    pad = CONV_W // 2
    return lax.conv_general_dilated(
        t, w[:, None, :].astype(t.dtype), window_strides=(1,), padding=((pad, pad),),
        dimension_numbers=('NWC', 'WIO', 'NWC'), feature_group_count=t.shape[-1])


def gated_head_norm(o, z, w):
    y = o * lax.rsqrt(jnp.mean(o * o, axis=-1, keepdims=True) + EPS) * w.astype(F32)
    return merge_heads(y) * jax.nn.silu(z.astype(F32))


def neighbourhood_attention(q, k, v, k_ctx, v_ctx, rpb, rows):
    bsz, n_tok, n_heads, dh = q.shape
    kr = min(WIN_ROWS, rows)
    n_cb = GRID_W // Q_COL_BLOCK
    scale = dh ** -0.5
    q = q.reshape(bsz, rows, n_cb, Q_COL_BLOCK, n_heads, dh)
    k = k.reshape(bsz, rows, GRID_W, n_heads, dh)
    v = v.reshape(bsz, rows, GRID_W, n_heads, dh)
    qcol = jnp.arange(GRID_W).reshape(n_cb, Q_COL_BLOCK)
    band = jnp.clip(qcol[:, :1] - WIN_COLS // 2, 0, GRID_W - KEY_COL_BAND) + jnp.arange(KEY_COL_BAND)
    win_start = jnp.clip(qcol - WIN_COLS // 2, 0, GRID_W - WIN_COLS)
    col_ok = (band[:, None, :] >= win_start[..., None]) & (band[:, None, :] < win_start[..., None] + WIN_COLS)
    dc_idx = jnp.clip(band[:, None, :] - qcol[..., None] + WIN_COLS - 1, 0, 2 * WIN_COLS - 2)
    rpb_f = rpb.astype(F32)

    def one_row(r):
        rs = jnp.clip(r - kr // 2, 0, rows - kr)
        k_rows = lax.dynamic_slice_in_dim(k, rs, kr, axis=1)
        v_rows = lax.dynamic_slice_in_dim(v, rs, kr, axis=1)
        k_band = k_rows[:, :, band]
        v_band = v_rows[:, :, band]
        q_r = lax.dynamic_index_in_dim(q, r, axis=1, keepdims=False)
        s_win = jnp.einsum('bnqhd,brnkhd->bhnqrk', q_r, k_band, preferred_element_type=F32) * scale
        dr_idx = rs + jnp.arange(kr) - r + WIN_ROWS - 1
        bias = rpb_f[:, dr_idx][:, :, dc_idx].transpose(0, 2, 3, 1, 4)
        s_win = jnp.where(col_ok[:, :, None, :], s_win + bias, -jnp.inf)
        s_win = s_win.reshape(bsz, n_heads, n_cb, Q_COL_BLOCK, kr * KEY_COL_BAND)
        s_ctx = jnp.einsum('bnqhd,blhd->bhnql', q_r, k_ctx, preferred_element_type=F32) * scale
        p = jax.nn.softmax(jnp.concatenate([s_win, s_ctx], axis=-1), axis=-1).astype(v.dtype)
        p_win = p[..., :kr * KEY_COL_BAND].reshape(bsz, n_heads, n_cb, Q_COL_BLOCK, kr, KEY_COL_BAND)
        p_ctx = p[..., kr * KEY_COL_BAND:]
        return (jnp.einsum('bhnqrk,brnkhd->bnqhd', p_win, v_band)
                + jnp.einsum('bhnql,blhd->bnqhd', p_ctx, v_ctx))

    out = lax.map(one_row, jnp.arange(rows))
    return jnp.moveaxis(out, 0, 1).reshape(bsz, n_tok, n_heads, dh)


def context_attention(q, k, v):
    s = jnp.einsum('blhd,bmhd->bhlm', q, k, preferred_element_type=F32) * q.shape[-1] ** -0.5
    p = jax.nn.softmax(s, axis=-1).astype(v.dtype)
    return jnp.einsum('bhlm,bmhd->blhd', p, v)


def bidirectional_with_prefix(scan_fn, lat_fwd, lat_bwd, ctx_fwd, ctx_bwd, s0):
    oc_f, sc_f = scan_fn(*ctx_fwd, s0)
    ol_f, _ = scan_fn(*lat_fwd, sc_f)
    oc_b, sc_b = scan_fn(*[rev(t) for t in ctx_bwd], s0)
    ol_b, _ = scan_fn(*[rev(t) for t in lat_bwd], sc_b)
    o_lat = ol_f + rev(ol_b)
    o_ctx = None if oc_f is None else oc_f + rev(oc_b)
    return o_lat, o_ctx


def chunk_gated_delta(q, k, v, g, beta, s0):
    bsz, n_tok, n_heads, _ = k.shape
    n = n_tok // CHUNK
    to_chunks = lambda t: t.reshape(bsz, n, CHUNK, n_heads, t.shape[-1]).transpose(1, 0, 3, 2, 4)
    kc, vc = to_chunks(k), to_chunks(v)
    gc = jnp.cumsum(g.reshape(bsz, n, CHUNK, n_heads).transpose(1, 0, 3, 2), axis=-1)
    bc = beta.reshape(bsz, n, CHUNK, n_heads).transpose(1, 0, 3, 2)
    causal = jnp.tril(jnp.ones((CHUNK, CHUNK), bool))
    strict = jnp.tril(jnp.ones((CHUNK, CHUNK), bool), -1)
    decay = jnp.exp(jnp.where(causal, gc[..., :, None] - gc[..., None, :], -jnp.inf))
    kb = kc * bc[..., None]
    lower = jnp.where(strict, jnp.einsum('nbhik,nbhjk->nbhij', kb, kc) * decay, 0.0)
    unit_lower = lower + jnp.eye(CHUNK, dtype=lower.dtype)
    u = lax.linalg.triangular_solve(unit_lower, vc * bc[..., None], left_side=True, lower=True, unit_diagonal=True)
    w = lax.linalg.triangular_solve(unit_lower, kb * jnp.exp(gc)[..., None], left_side=True, lower=True,
                                    unit_diagonal=True)
    g_last = gc[..., -1]
    k_dec = kc * jnp.exp(g_last[..., None] - gc)[..., None]

    def new_values(s, u_i, w_i):
        return u_i - jnp.einsum('bhck,bhkv->bhcv', w_i, s)

    def update(s, v_new, kd_i, gl_i):
        return s * jnp.exp(gl_i)[..., None, None] + jnp.einsum('bhck,bhcv->bhkv', kd_i, v_new)

    if q is None:
        def state_step(s, xs):
            u_i, w_i, kd_i, gl_i = xs
            return update(s, new_values(s, u_i, w_i), kd_i, gl_i), None
        s_fin, _ = lax.scan(state_step, s0, (u, w, k_dec, g_last))
        return None, s_fin

    qc = to_chunks(q)
    q_dec = qc * jnp.exp(gc)[..., None]
    qk = jnp.where(causal, jnp.einsum('nbhik,nbhjk->nbhij', qc, kc) * decay, 0.0)

    def step(s, xs):
        u_i, w_i, kd_i, gl_i, qd_i, qk_i = xs
        v_new = new_values(s, u_i, w_i)
        o = jnp.einsum('bhck,bhkv->bhcv', qd_i, s) + jnp.einsum('bhij,bhjv->bhiv', qk_i, v_new)
        return update(s, v_new, kd_i, gl_i), o

    s_fin, o = lax.scan(step, s0, (u, w, k_dec, g_last, q_dec, qk))
    return o.transpose(1, 0, 3, 2, 4).reshape(bsz, n_tok, n_heads, -1), s_fin


def chunk_gla(q, k, v, log_a, s0):
    bsz, n_tok, n_heads, _ = k.shape
    n = n_tok // CHUNK
    to_chunks = lambda t: t.reshape(bsz, n, CHUNK, n_heads, t.shape[-1]).transpose(1, 0, 3, 2, 4)
    kc, vc = to_chunks(k), to_chunks(v)
    b = jnp.cumsum(to_chunks(log_a), axis=-2)
    b_last = b[..., -1:, :]
    k_dec = kc * jnp.exp(b_last - b)
    d_last = jnp.exp(b_last[..., 0, :])

    def update(s, k_i, v_i, d_i):
        return d_i[..., None] * s + jnp.einsum('bhck,bhcv->bhkv', k_i, v_i)

    if q is None:
        s_fin, _ = lax.scan(lambda s, xs: (update(s, *xs), None), s0, (k_dec, vc, d_last))
        return None, s_fin

    q_dec = to_chunks(q) * jnp.exp(b)
    k_inv = kc * jnp.exp(-b)
    causal = jnp.tril(jnp.ones((CHUNK, CHUNK), bool))
    scores = jnp.where(causal, jnp.einsum('nbhik,nbhjk->nbhij', q_dec, k_inv), 0.0)
    o_intra = jnp.einsum('nbhij,nbhjv->nbhiv', scores, vc)

    def step(s, xs):
        q_i, k_i, v_i, d_i = xs
        return update(s, k_i, v_i, d_i), jnp.einsum('bhck,bhkv->bhcv', q_i, s)

    s_fin, o_inter = lax.scan(step, s0, (q_dec, k_dec, vc, d_last))
    o = (o_intra + o_inter).transpose(1, 0, 3, 2, 4).reshape(bsz, n_tok, n_heads, -1)
    return o, s_fin


def gdn_prepare(pg, conv_w, a_log, dt_bias):
    bsz, n_tok, _ = pg['gdn_k'].shape
    conv_q, conv_k, conv_v = conv_w[:, :GDN_KW], conv_w[:, GDN_KW:2 * GDN_KW], conv_w[:, 2 * GDN_KW:]

    def conv_act(t, w):
        return split_heads(jax.nn.silu(centred_conv(t.astype(F32), w)), GDN_HEADS)

    k = l2_normalize(conv_act(pg['gdn_k'], conv_k))
    v = conv_act(pg['gdn_v'], conv_v)
    q = l2_normalize(conv_act(pg['gdn_q'], conv_q)) * GDN_DK ** -0.5 if 'gdn_q' in pg else None
    beta = jax.nn.sigmoid(pg['gdn_beta'].astype(F32)).reshape(bsz, n_tok, 2, GDN_HEADS)
    g = -jnp.exp(a_log.astype(F32)) * jax.nn.softplus(
        pg['gdn_a'].astype(F32).reshape(bsz, n_tok, 2, GDN_HEADS) + dt_bias.astype(F32))
    return q, k, v, beta, g


def gdn_mixer(pl_, pc, conv_w, a_log, dt_bias, norm_w, ctx_out):
    ql, kl, vl, bl, gl = gdn_prepare(pl_, conv_w, a_log, dt_bias)
    qc, kc, vc, bc, gc = gdn_prepare(pc, conv_w, a_log, dt_bias)
    s0 = jnp.zeros((kl.shape[0], GDN_HEADS, GDN_DK, GDN_DV), F32)
    o_lat, o_ctx = bidirectional_with_prefix(
        chunk_gated_delta,
        (ql, kl, vl, gl[:, :, 0], bl[:, :, 0]), (ql, kl, vl, gl[:, :, 1], bl[:, :, 1]),
        (qc, kc, vc, gc[:, :, 0], bc[:, :, 0]), (qc, kc, vc, gc[:, :, 1], bc[:, :, 1]), s0)
    y = gated_head_norm(o_lat, pl_['gdn_z'], norm_w)
    yc = gated_head_norm(o_ctx, pc['gdn_z'], norm_w) if ctx_out else None
    return y, yc


def gla_prepare(pg, w2, b2, ang):
    bsz, n_tok, _ = pg['gla_k'].shape

    def qk_heads(t):
        t = split_heads(t.astype(F32), GLA_HEADS)
        if ang is not None:
            t = apply_rope2d(t, ang)
        return t

    k = qk_heads(pg['gla_k'])
    q = qk_heads(pg['gla_q']) * GLA_DK ** -0.5 if 'gla_q' in pg else None
    v = split_heads(pg['gla_v'], GLA_HEADS).astype(F32)
    lr = pg['gla_lr'].astype(F32).reshape(bsz, n_tok, 2, GLA_LOWRANK)
    log_a = jax.nn.log_sigmoid(jnp.einsum('btzr,zrk->btzk', lr, w2.astype(F32)) + b2.astype(F32)) / GLA_GATE_NORM
    return q, k, v, log_a.reshape(bsz, n_tok, 2, GLA_HEADS, GLA_DK)


def gla_mixer(pl_, pc, w2, b2, norm_w, ang, ctx_out):
    ql, kl, vl, al = gla_prepare(pl_, w2, b2, ang)
    qc, kc, vc, ac = gla_prepare(pc, w2, b2, None)
    s0 = jnp.zeros((kl.shape[0], GLA_HEADS, GLA_DK, GLA_DV), F32)
    o_lat, o_ctx = bidirectional_with_prefix(
        chunk_gla,
        (ql, kl, vl, al[:, :, 0]), (ql, kl, vl, al[:, :, 1]),
        (qc, kc, vc, ac[:, :, 0]), (qc, kc, vc, ac[:, :, 1]), s0)
    y = gated_head_norm(o_lat, pl_['gla_z'], norm_w)
    yc = gated_head_norm(o_ctx, pc['gla_z'], norm_w) if ctx_out else None
    return y, yc


def na_mixer(pl_, pc, rpb, rows, ctx_out):
    k_c = split_heads(pc['na_k'].astype(F32), NA_HEADS)
    v_c = split_heads(pc['na_v'].astype(F32), NA_HEADS)
    y = neighbourhood_attention(split_heads(pl_['na_q'].astype(F32), NA_HEADS),
                                split_heads(pl_['na_k'].astype(F32), NA_HEADS),
                                split_heads(pl_['na_v'].astype(F32), NA_HEADS), k_c, v_c, rpb, rows)
    y = merge_heads(y)
    yc = merge_heads(context_attention(split_heads(pc['na_q'].astype(F32), NA_HEADS), k_c, v_c)) if ctx_out else None
    return y, yc


def split_groups(proj, narrow, drop=()):
    out = {}
    for name in WIDE_GROUPS:
        if name not in drop:
            off, width = WIDE_OFFSET[name]
            out[name] = proj[..., off:off + width]
    for name in NARROW_GROUPS:
        off, width = NARROW_OFFSET[name]
        out[name] = narrow[..., off:off + width]
    return out


def expert_choice_ffn(h, logits, w_gate, w_up, w_down):
    bsz, n_tok, d = h.shape
    cap = max(1, CAPACITY_FACTOR * n_tok // N_EXPERTS)
    affinity = jax.nn.softmax(logits, axis=-1)
    gate, idx = lax.top_k(jnp.swapaxes(affinity, 1, 2), cap)
    xs = jax.vmap(lambda hb, ib: hb[ib])(h, idx)
    ys = expert_ffn(xs, gate[..., None], w_gate, w_up, w_down, tc=512)
    return jax.vmap(lambda yb, ib: jnp.zeros((n_tok, d), F32).at[ib.reshape(-1)].add(yb.reshape(-1, d)))(ys, idx)


def pack_in_weight(w_in):
    wide = jnp.concatenate([w_in[:, IN_OFFSET[n][0]:IN_OFFSET[n][0] + IN_OFFSET[n][1]] for n in WIDE_GROUPS], axis=1)
    narrow = jnp.concatenate([w_in[:, IN_OFFSET[n][0]:IN_OFFSET[n][0] + IN_OFFSET[n][1]] for n in NARROW_GROUPS],
                             axis=1)
    narrow = jnp.pad(narrow, ((0, 0), (0, NARROW_WIDTH - narrow.shape[1])))
    return wide.astype(BF16), narrow.astype(BF16)


def kernel(x, c, ctx, c_ctx, norm1_w, norm2_w, final_norm_w, w_ada, b_ada, w_in, na_rpb, gdn_conv,
           gdn_a_log, gdn_dt_bias, gdn_norm_w, gla_w2, gla_b2, gla_norm_w, w_branch, w_out,
           w_router, w_gate, w_up, w_down):
    bsz, n_lat, d = x.shape
    rows = n_lat // GRID_W
    ang = axial_rope_angles(n_lat, GLA_DK)
    cond_lat = jax.nn.silu(c)
    cond_ctx = jnp.broadcast_to(jax.nn.silu(c_ctx)[None], (bsz, d))
    for layer in range(DEPTH):
        ctx_out = layer < DEPTH - 1
        mod_lat = jnp.split((cond_lat @ w_ada[layer] + b_ada[layer])[:, None, :], 6, axis=-1)
        mod_ctx = jnp.split((cond_ctx @ w_ada[layer] + b_ada[layer])[:, None, :], 6, axis=-1)
        w_wide, w_narrow = pack_in_weight(w_in[layer])
        wb = w_branch[layer].astype(BF16)
        wo = w_out[layer].astype(BF16)
        wg, wu, wd = w_gate[layer].astype(BF16), w_up[layer].astype(BF16), w_down[layer].astype(BF16)
        wr = jnp.pad(w_router[layer], ((0, 0), (0, LANES - N_EXPERTS)))

        proj_l, narrow_l = norm_mod_project(x, norm1_w[layer], mod_lat[0], mod_lat[1], w_wide, w_narrow,
                                            tm=1024, tn=1024)
        proj_c, narrow_c = norm_mod_project(ctx, norm1_w[layer], mod_ctx[0], mod_ctx[1], w_wide, w_narrow,
                                            tm=256, tn=1024)
        pl_ = split_groups(proj_l, narrow_l)
        pc = split_groups(proj_c, narrow_c)
        if not ctx_out:
            for name in ('gdn_q', 'gla_q'):
                pc.pop(name)

        ya, yca = na_mixer(pl_, pc, na_rpb[layer], rows, ctx_out)
        yb, ycb = gdn_mixer(pl_, pc, gdn_conv[layer], gdn_a_log[layer], gdn_dt_bias[layer], gdn_norm_w[layer],
                            ctx_out)
        yg, ycg = gla_mixer(pl_, pc, gla_w2[layer], gla_b2[layer], gla_norm_w[layer], ang, ctx_out)

        y = merge_branches(proj_l, ya.astype(BF16), yb.astype(BF16), yg.astype(BF16), wb, tm=1024, tn=1024)
        x = out_proj_residual(y, wo, x, mod_lat[2], tm=1024, tn=1024)
        h2, logits = norm_mod_router(x, norm2_w[layer], mod_lat[3], mod_lat[4], wr, tm=512)
        x = x + mod_lat[5] * expert_choice_ffn(h2, logits[..., :N_EXPERTS], wg, wu, wd)
        if ctx_out:
            yc = merge_branches(proj_c, yca.astype(BF16), ycb.astype(BF16), ycg.astype(BF16), wb, tm=256, tn=1024)
            ctx = out_proj_residual(yc, wo, ctx, mod_ctx[2], tm=256, tn=1024)
            hc2, logits_c = norm_mod_router(ctx, norm2_w[layer], mod_ctx[3], mod_ctx[4], wr, tm=256)
            ctx = ctx + mod_ctx[5] * expert_choice_ffn(hc2, logits_c[..., :N_EXPERTS], wg, wu, wd)
    return final_norm(x, final_norm_w, tm=512)
```

```python
import functools

import jax
import jax.numpy as jnp
from jax import lax
from jax.experimental import pallas as pl
from jax.experimental.pallas import tpu as pltpu

D_MODEL = 2048
DEPTH = 4
GRID_W = 64

NA_HEADS = 8
NA_DIM = 128
WIN_ROWS = 8
WIN_COLS = 16
Q_COL_BLOCK = 16
KEY_COL_BAND = 32

GDN_HEADS = 8
GDN_DK = 128
GDN_DV = 128
CONV_W = 5

GLA_HEADS = 4
GLA_DK = 128
GLA_DV = 256
GLA_LOWRANK = 16
GLA_GATE_NORM = 16.0

CHUNK = 64
ROPE_BASE = 10000.0

N_BRANCH = 3
BRANCH_W = 1024

N_EXPERTS = 16
EXPERT_FF = 1024
CAPACITY_FACTOR = 2

EPS = 1e-6
F32 = jnp.float32
BF16 = jnp.bfloat16

NA_W = NA_HEADS * NA_DIM
GDN_KW = GDN_HEADS * GDN_DK
GDN_VW = GDN_HEADS * GDN_DV
GLA_KW = GLA_HEADS * GLA_DK
GLA_VW = GLA_HEADS * GLA_DV

IN_GROUPS = (
    ('na_q', NA_W), ('na_k', NA_W), ('na_v', NA_W),
    ('gdn_q', GDN_KW), ('gdn_k', GDN_KW), ('gdn_v', GDN_VW), ('gdn_z', GDN_VW),
    ('gdn_beta', 2 * GDN_HEADS), ('gdn_a', 2 * GDN_HEADS),
    ('gla_q', GLA_KW), ('gla_k', GLA_KW), ('gla_v', GLA_VW), ('gla_z', GLA_VW),
    ('gla_lr', 2 * GLA_LOWRANK),
    ('gate_na', D_MODEL), ('gate_gdn', D_MODEL), ('gate_gla', D_MODEL),
)
IN_OFFSET = {}
_off = 0
for _name, _width in IN_GROUPS:
    IN_OFFSET[_name] = (_off, _width)
    _off += _width

WIDE_GROUPS = ('na_q', 'na_k', 'na_v', 'gdn_q', 'gdn_k', 'gdn_v', 'gdn_z',
               'gla_q', 'gla_k', 'gla_v', 'gla_z', 'gate_na', 'gate_gdn', 'gate_gla')
NARROW_GROUPS = ('gdn_beta', 'gdn_a', 'gla_lr')
WIDE_OFFSET = {}
_off = 0
for _name in WIDE_GROUPS:
    WIDE_OFFSET[_name] = (_off, IN_OFFSET[_name][1])
    _off += IN_OFFSET[_name][1]
WIDE_WIDTH = _off
NARROW_OFFSET = {}
_off = 0
for _name in NARROW_GROUPS:
    NARROW_OFFSET[_name] = (_off, IN_OFFSET[_name][1])
    _off += IN_OFFSET[_name][1]
LANES = 128
NARROW_WIDTH = LANES

VMEM_LIMIT = 56 * 1024 * 1024


def _cparams(*sem):
    return pltpu.CompilerParams(dimension_semantics=sem, vmem_limit_bytes=VMEM_LIMIT)


def _norm_mod_rows(x, nw, shift, scale):
    y = x * lax.rsqrt(jnp.mean(x * x, axis=-1, keepdims=True) + EPS) * nw
    return y * (1.0 + scale) + shift


def _proj_kernel(x_ref, nw_ref, sh_ref, sc_ref, w_ref, wn_ref, o_ref, on_ref, h_scr, *, row_chunk):
    n = pl.program_id(2)
    tm = x_ref.shape[1]

    @pl.when(n == 0)
    def _():
        def body(i, carry):
            rows = pl.ds(pl.multiple_of(i * row_chunk, row_chunk), row_chunk)
            h = _norm_mod_rows(x_ref[0, rows, :], nw_ref[...], sh_ref[0], sc_ref[0])
            h_scr[rows, :] = h.astype(BF16)
            return carry
        lax.fori_loop(0, tm // row_chunk, body, 0)
        on_ref[0] = jnp.dot(h_scr[...], wn_ref[...], preferred_element_type=F32)

    o_ref[0] = jnp.dot(h_scr[...], w_ref[...], preferred_element_type=F32).astype(o_ref.dtype)


def norm_mod_project(x, nw, shift, scale, w_wide, w_narrow, *, tm, tn):
    bsz, n_tok, d = x.shape
    n_wide = w_wide.shape[1]
    n_narrow = w_narrow.shape[1]
    tm = min(tm, n_tok)
    row_chunk = min(128, tm)
    grid = (bsz, n_tok // tm, n_wide // tn)
    return pl.pallas_call(
        functools.partial(_proj_kernel, row_chunk=row_chunk),
        grid=grid,
        in_specs=[
            pl.BlockSpec((1, tm, d), lambda b, m, n: (b, m, 0)),
            pl.BlockSpec((1, d), lambda b, m, n: (0, 0)),
            pl.BlockSpec((1, 1, d), lambda b, m, n: (b, 0, 0)),
            pl.BlockSpec((1, 1, d), lambda b, m, n: (b, 0, 0)),
            pl.BlockSpec((d, tn), lambda b, m, n: (0, n)),
            pl.BlockSpec((d, n_narrow), lambda b, m, n: (0, 0)),
        ],
        out_specs=[
            pl.BlockSpec((1, tm, tn), lambda b, m, n: (b, m, n)),
            pl.BlockSpec((1, tm, n_narrow), lambda b, m, n: (b, m, 0)),
        ],
        out_shape=[
            jax.ShapeDtypeStruct((bsz, n_tok, n_wide), BF16),
            jax.ShapeDtypeStruct((bsz, n_tok, n_narrow), F32),
        ],
        scratch_shapes=[pltpu.VMEM((tm, d), BF16)],
        compiler_params=_cparams("parallel", "parallel", "arbitrary"),
        name="norm_mod_project",
    )(x, nw.reshape(1, d), shift, scale, w_wide, w_narrow)


def _merge_kernel(ga_ref, gb_ref, gg_ref, ya_ref, yb_ref, yg_ref, wb_ref, o_ref):
    acc = jax.nn.sigmoid(ga_ref[0].astype(F32)) * jnp.dot(ya_ref[0], wb_ref[0], preferred_element_type=F32)
    acc += jax.nn.sigmoid(gb_ref[0].astype(F32)) * jnp.dot(yb_ref[0], wb_ref[1], preferred_element_type=F32)
    acc += jax.nn.sigmoid(gg_ref[0].astype(F32)) * jnp.dot(yg_ref[0], wb_ref[2], preferred_element_type=F32)
    o_ref[0] = acc.astype(o_ref.dtype)


def merge_branches(proj, ya, yb, yg, w_branch, *, tm, tn):
    bsz, n_tok, _ = proj.shape
    d = w_branch.shape[2]
    tm = min(tm, n_tok)
    gate_blk = [WIDE_OFFSET[name][0] // tn for name in ('gate_na', 'gate_gdn', 'gate_gla')]

    def gate_spec(blk):
        return pl.BlockSpec((1, tm, tn), lambda b, m, n: (b, m, blk + n))

    br_spec = pl.BlockSpec((1, tm, BRANCH_W), lambda b, m, n: (b, m, 0))
    return pl.pallas_call(
        _merge_kernel,
        grid=(bsz, n_tok // tm, d // tn),
        in_specs=[gate_spec(gate_blk[0]), gate_spec(gate_blk[1]), gate_spec(gate_blk[2]),
                  br_spec, br_spec, br_spec,
                  pl.BlockSpec((N_BRANCH, BRANCH_W, tn), lambda b, m, n: (0, 0, n))],
        out_specs=pl.BlockSpec((1, tm, tn), lambda b, m, n: (b, m, n)),
        out_shape=jax.ShapeDtypeStruct((bsz, n_tok, d), BF16),
        compiler_params=_cparams("parallel", "parallel", "arbitrary"),
        name="merge_branches",
    )(proj, proj, proj, ya, yb, yg, w_branch)


def _out_residual_kernel(y_ref, w_ref, x_ref, g_ref, o_ref):
    o_ref[0] = x_ref[0] + g_ref[0] * jnp.dot(y_ref[0], w_ref[...], preferred_element_type=F32)


def out_proj_residual(y, w_out, x, gate, *, tm, tn):
    bsz, n_tok, d = x.shape
    tm = min(tm, n_tok)
    return pl.pallas_call(
        _out_residual_kernel,
        grid=(bsz, n_tok // tm, d // tn),
        in_specs=[
            pl.BlockSpec((1, tm, d), lambda b, m, n: (b, m, 0)),
            pl.BlockSpec((d, tn), lambda b, m, n: (0, n)),
            pl.BlockSpec((1, tm, tn), lambda b, m, n: (b, m, n)),
            pl.BlockSpec((1, 1, tn), lambda b, m, n: (b, 0, n)),
        ],
        out_specs=pl.BlockSpec((1, tm, tn), lambda b, m, n: (b, m, n)),
        out_shape=jax.ShapeDtypeStruct((bsz, n_tok, d), F32),
        compiler_params=_cparams("parallel", "parallel", "arbitrary"),
        name="out_proj_residual",
    )(y, w_out, x, gate)


def _norm_router_kernel(x_ref, nw_ref, sh_ref, sc_ref, wr_ref, h_ref, l_ref):
    h = _norm_mod_rows(x_ref[0], nw_ref[...], sh_ref[0], sc_ref[0])
    h_ref[0] = h.astype(h_ref.dtype)
    l_ref[0] = jnp.dot(h, wr_ref[...], preferred_element_type=F32, precision=lax.Precision.HIGHEST)


def norm_mod_router(x, nw, shift, scale, w_router_pad, *, tm):
    bsz, n_tok, d = x.shape
    n_pad = w_router_pad.shape[1]
    tm = min(tm, n_tok)
    return pl.pallas_call(
        _norm_router_kernel,
        grid=(bsz, n_tok // tm),
        in_specs=[
            pl.BlockSpec((1, tm, d), lambda b, m: (b, m, 0)),
            pl.BlockSpec((1, d), lambda b, m: (0, 0)),
            pl.BlockSpec((1, 1, d), lambda b, m: (b, 0, 0)),
            pl.BlockSpec((1, 1, d), lambda b, m: (b, 0, 0)),
            pl.BlockSpec((d, n_pad), lambda b, m: (0, 0)),
        ],
        out_specs=[
            pl.BlockSpec((1, tm, d), lambda b, m: (b, m, 0)),
            pl.BlockSpec((1, tm, n_pad), lambda b, m: (b, m, 0)),
        ],
        out_shape=[
            jax.ShapeDtypeStruct((bsz, n_tok, d), BF16),
            jax.ShapeDtypeStruct((bsz, n_tok, n_pad), F32),
        ],
        compiler_params=_cparams("parallel", "parallel"),
        name="norm_mod_router",
    )(x, nw.reshape(1, d), shift, scale, w_router_pad)


def _expert_ffn_kernel(xs_ref, g_ref, wg_ref, wu_ref, wd_ref, o_ref):
    xs = xs_ref[0, 0]
    hid = jax.nn.silu(jnp.dot(xs, wg_ref[0], preferred_element_type=F32)) * jnp.dot(
        xs, wu_ref[0], preferred_element_type=F32)
    ys = jnp.dot(hid.astype(BF16), wd_ref[0], preferred_element_type=F32)
    o_ref[0, 0] = ys * g_ref[0, 0]


def expert_ffn(xs, gate, w_gate, w_up, w_down, *, tc):
    bsz, n_exp, cap, d = xs.shape
    ff = w_gate.shape[2]
    tc = min(tc, cap)
    return pl.pallas_call(
        _expert_ffn_kernel,
        grid=(n_exp, bsz, cap // tc),
        in_specs=[
            pl.BlockSpec((1, 1, tc, d), lambda e, b, c: (b, e, c, 0)),
            pl.BlockSpec((1, 1, tc, 1), lambda e, b, c: (b, e, c, 0)),
            pl.BlockSpec((1, d, ff), lambda e, b, c: (e, 0, 0)),
            pl.BlockSpec((1, d, ff), lambda e, b, c: (e, 0, 0)),
            pl.BlockSpec((1, ff, d), lambda e, b, c: (e, 0, 0)),
        ],
        out_specs=pl.BlockSpec((1, 1, tc, d), lambda e, b, c: (b, e, c, 0)),
        out_shape=jax.ShapeDtypeStruct((bsz, n_exp, cap, d), F32),
        compiler_params=_cparams("parallel", "parallel", "arbitrary"),
        name="expert_ffn",
    )(xs, gate, w_gate, w_up, w_down)


def _final_norm_kernel(x_ref, w_ref, o_ref):
    x = x_ref[0]
    o_ref[0] = x * lax.rsqrt(jnp.mean(x * x, axis=-1, keepdims=True) + EPS) * w_ref[...]


def final_norm(x, w, *, tm):
    bsz, n_tok, d = x.shape
    return pl.pallas_call(
        _final_norm_kernel,
        grid=(bsz, n_tok // tm),
        in_specs=[pl.BlockSpec((1, tm, d), lambda b, m: (b, m, 0)),
                  pl.BlockSpec((1, d), lambda b, m: (0, 0))],
        out_specs=pl.BlockSpec((1, tm, d), lambda b, m: (b, m, 0)),
        out_shape=jax.ShapeDtypeStruct((bsz, n_tok, d), F32),
        compiler_params=_cparams("parallel", "parallel"),
        name="final_norm",
    )(x, w.reshape(1, d))


NA_Q_ROWS = 8
NA_KEY_SLOTS = 4
NA_KV_ROWS = 4
NA_Q_TOK = NA_Q_ROWS * GRID_W
NA_KV_TOK = NA_KV_ROWS * GRID_W
NA_Q_CHUNK = 128


def na_bias_tables(rpb, rows):
    n_steps = rows // NA_Q_ROWS
    qi = jnp.arange(NA_Q_TOK)
    pi = jnp.arange(NA_KEY_SLOTS * NA_KV_TOK)
    tables = []
    for g in (0, 1, n_steps - 1):
        r = NA_Q_ROWS * g + qi // GRID_W
        c = qi % GRID_W
        kr = NA_Q_ROWS * g - WIN_ROWS // 2 + pi // GRID_W
        kc = pi % GRID_W
        rs = jnp.clip(r - WIN_ROWS // 2, 0, rows - WIN_ROWS)
        ws = jnp.clip(c - WIN_COLS // 2, 0, GRID_W - WIN_COLS)
        valid = ((kr[None, :] >= rs[:, None]) & (kr[None, :] < rs[:, None] + WIN_ROWS)
                 & (kc[None, :] >= ws[:, None]) & (kc[None, :] < ws[:, None] + WIN_COLS))
        dr = jnp.clip(kr[None, :] - r[:, None] + WIN_ROWS - 1, 0, 2 * WIN_ROWS - 2)
        dc = jnp.clip(kc[None, :] - c[:, None] + WIN_COLS - 1, 0, 2 * WIN_COLS - 2)
        tables.append(jnp.where(valid[None], rpb.astype(F32)[:, dr, dc], -jnp.inf))
    return jnp.stack(tables)


def _na_kernel(q_ref, k0, k1, k2, k3, v0, v1, v2, v3, kc_ref, vc_ref, tbl_ref, o_ref):
    scale = NA_DIM ** -0.5
    k_refs = (k0, k1, k2, k3, kc_ref)
    v_refs = (v0, v1, v2, v3, vc_ref)
    contract_last = (((1,), (1,)), ((), ()))

    def body(i, carry):
        rows = pl.ds(pl.multiple_of(i * NA_Q_CHUNK, NA_Q_CHUNK), NA_Q_CHUNK)
        q = q_ref[0, rows, :]
        scores = []
        for s, k_ref in enumerate(k_refs):
            sc = lax.dot_general(q, k_ref[0], contract_last, preferred_element_type=F32) * scale
            if s < NA_KEY_SLOTS:
                sc = sc + tbl_ref[0, 0, rows, s * NA_KV_TOK:(s + 1) * NA_KV_TOK]
            scores.append(sc)
        m = scores[0].max(axis=-1, keepdims=True)
        for sc in scores[1:]:
            m = jnp.maximum(m, sc.max(axis=-1, keepdims=True))
        probs = [jnp.exp(sc - m) for sc in scores]
        denom = probs[0].sum(axis=-1, keepdims=True)
        for p in probs[1:]:
            denom = denom + p.sum(axis=-1, keepdims=True)
        inv = 1.0 / denom
        acc = jnp.zeros((NA_Q_CHUNK, NA_DIM), F32)
        for p, v_ref in zip(probs, v_refs):
            acc = acc + jnp.dot((p * inv).astype(BF16), v_ref[0], preferred_element_type=F32)
        o_ref[0, rows, :] = acc.astype(o_ref.dtype)
        return carry

    lax.fori_loop(0, NA_Q_TOK // NA_Q_CHUNK, body, 0)


def neighbourhood_attention(proj_l, proj_c, rpb, rows):
    bsz, n_tok, _ = proj_l.shape
    n_ctx = proj_c.shape[1]
    n_steps = rows // NA_Q_ROWS
    n_kv_blocks = n_tok // NA_KV_TOK
    assert n_steps >= 2 and rows % NA_Q_ROWS == 0
    tables = na_bias_tables(rpb, rows)
    q_blk, k_blk, v_blk = (WIDE_OFFSET[n][0] // NA_DIM for n in ('na_q', 'na_k', 'na_v'))

    def kv_spec(col_blk, slot):
        def idx(h, b, g):
            blk = jnp.clip(2 * g - 1 + slot, 0, n_kv_blocks - 1)
            return (b, blk, col_blk + h)
        return pl.BlockSpec((1, NA_KV_TOK, NA_DIM), idx)

    def tbl_idx(h, b, g):
        kind = jnp.where(g == 0, 0, jnp.where(g == n_steps - 1, 2, 1))
        return (kind, h, 0, 0)

    return pl.pallas_call(
        _na_kernel,
        grid=(NA_HEADS, bsz, n_steps),
        in_specs=[pl.BlockSpec((1, NA_Q_TOK, NA_DIM), lambda h, b, g: (b, g, q_blk + h))]
        + [kv_spec(k_blk, s) for s in range(NA_KEY_SLOTS)]
        + [kv_spec(v_blk, s) for s in range(NA_KEY_SLOTS)]
        + [pl.BlockSpec((1, n_ctx, NA_DIM), lambda h, b, g: (b, 0, k_blk + h)),
           pl.BlockSpec((1, n_ctx, NA_DIM), lambda h, b, g: (b, 0, v_blk + h)),
           pl.BlockSpec((1, 1, NA_Q_TOK, NA_KEY_SLOTS * NA_KV_TOK), tbl_idx)],
        out_specs=pl.BlockSpec((1, NA_Q_TOK, NA_DIM), lambda h, b, g: (b, g, h)),
        out_shape=jax.ShapeDtypeStruct((bsz, n_tok, NA_W), BF16),
        compiler_params=_cparams("parallel", "parallel", "arbitrary"),
        name="neighbourhood_attention",
    )(proj_l, *([proj_l] * (2 * NA_KEY_SLOTS)), proj_c, proj_c, tables)


def _ctx_attn_kernel(q_ref, k_ref, v_ref, o_ref):
    s = lax.dot_general(q_ref[0], k_ref[0], (((1,), (1,)), ((), ())), preferred_element_type=F32) * NA_DIM ** -0.5
    p = jnp.exp(s - s.max(axis=-1, keepdims=True))
    p = p / p.sum(axis=-1, keepdims=True)
    o_ref[0] = jnp.dot(p.astype(BF16), v_ref[0], preferred_element_type=F32).astype(o_ref.dtype)


def context_attention(proj_c):
    bsz, n_ctx, _ = proj_c.shape
    q_blk, k_blk, v_blk = (WIDE_OFFSET[n][0] // NA_DIM for n in ('na_q', 'na_k', 'na_v'))

    def spec(col_blk):
        return pl.BlockSpec((1, n_ctx, NA_DIM), lambda b, h: (b, 0, col_blk + h))

    return pl.pallas_call(
        _ctx_attn_kernel,
        grid=(bsz, NA_HEADS),
        in_specs=[spec(q_blk), spec(k_blk), spec(v_blk)],
        out_specs=pl.BlockSpec((1, n_ctx, NA_DIM), lambda b, h: (b, 0, h)),
        out_shape=jax.ShapeDtypeStruct((bsz, n_ctx, NA_W), BF16),
        compiler_params=_cparams("parallel", "parallel"),
        name="context_attention",
    )(proj_c, proj_c, proj_c)


HIGHEST = lax.Precision.HIGHEST
NT_DIMS = (((1,), (1,)), ((), ()))
TN_DIMS = (((0,), (0,)), ((), ()))
SCAN_TB = 512


def _scan_masks(d):
    r = lax.broadcasted_iota(jnp.int32, (CHUNK, CHUNK), 0)
    c = lax.broadcasted_iota(jnp.int32, (CHUNK, CHUNK), 1)
    diff = (r - c) * jnp.where(d == 0, 1, -1)
    return diff >= 0, diff > 0


def _softplus(x):
    return jnp.maximum(x, 0.0) + jnp.log(1.0 + jnp.exp(-jnp.abs(x)))


def _gdn_conv_kernel(x_ref, prev_ref, next_ref, w_ref, o_ref, ext_scr):
    m = pl.program_id(1)
    s = pl.program_id(2)
    n_m = pl.num_programs(1)
    tm = x_ref.shape[1]
    halo = 8
    pad = CONV_W // 2
    prev = prev_ref[0].astype(F32)[halo:, :]
    nxt = next_ref[0].astype(F32)[:halo, :]
    ext_scr[0:halo, :] = jnp.where(m > 0, prev, 0.0)
    ext_scr[halo:halo + tm, :] = x_ref[0].astype(F32)
    ext_scr[halo + tm:2 * halo + tm, :] = jnp.where(m < n_m - 1, nxt, 0.0)
    q_scale = jnp.where(s == 0, GDN_DK ** -0.5, 1.0)
    for h in range(GDN_HEADS):
        hs = slice(h * GDN_DK, (h + 1) * GDN_DK)
        acc = jnp.zeros((tm, GDN_DK), F32)
        for j in range(CONV_W):
            acc = acc + ext_scr[pl.ds(halo - pad + j, tm), hs] * w_ref[j:j + 1, hs]
        y = acc * jax.nn.sigmoid(acc)
        normed = y * lax.rsqrt(jnp.sum(y * y, axis=-1, keepdims=True) + EPS) * q_scale
        o_ref[0, :, hs] = jnp.where(s < 2, normed, y).astype(o_ref.dtype)


def gdn_conv_act(proj, conv_w8, *, tm):
    bsz, n_tok, _ = proj.shape
    tm = min(tm, n_tok)
    halo_rows = 16
    col0 = WIDE_OFFSET['gdn_q'][0] // GDN_KW
    n_halo = n_tok // halo_rows
    per = tm // halo_rows
    return pl.pallas_call(
        _gdn_conv_kernel,
        grid=(bsz, n_tok // tm, 3),
        in_specs=[
            pl.BlockSpec((1, tm, GDN_KW), lambda b, m, s: (b, m, col0 + s)),
            pl.BlockSpec((1, halo_rows, GDN_KW), lambda b, m, s: (b, jnp.maximum(m * per - 1, 0), col0 + s)),
            pl.BlockSpec((1, halo_rows, GDN_KW), lambda b, m, s: (b, jnp.minimum((m + 1) * per, n_halo - 1), col0 + s)),
            pl.BlockSpec((8, GDN_KW), lambda b, m, s: (0, s)),
        ],
        out_specs=pl.BlockSpec((1, tm, GDN_KW), lambda b, m, s: (b, m, s)),
        out_shape=jax.ShapeDtypeStruct((bsz, n_tok, 3 * GDN_KW), BF16),
        scratch_shapes=[pltpu.VMEM((tm + 16, GDN_KW), F32)],
        compiler_params=_cparams("parallel", "parallel", "arbitrary"),
        name="gdn_conv",
    )(proj, proj, proj, conv_w8)


def _gdn_feat_kernel(n_ref, alog_ref, dt_ref, sel_ref, f_ref):
    x = n_ref[0]
    tm = x.shape[0]
    beta = jax.nn.sigmoid(x)
    g = -jnp.exp(alog_ref[...]) * _softplus(x + dt_ref[...])
    r = lax.broadcasted_iota(jnp.int32, (tm, tm), 0)
    c = lax.broadcasted_iota(jnp.int32, (tm, tm), 1)
    same = (r // CHUNK) == (c // CHUNK)
    dot = functools.partial(jnp.dot, precision=HIGHEST, preferred_element_type=F32)
    prefix = dot((same & (r >= c)).astype(F32), g)
    suffix = dot((same & (r <= c)).astype(F32), g)
    total = dot(same.astype(F32), g)
    for d, cum in enumerate((prefix, suffix)):
        f_ref[d, 0] = dot(beta, sel_ref[d, 0]) + dot(cum, sel_ref[d, 1]) + dot(total, sel_ref[d, 2])


def gdn_feature_select():
    sel = jnp.zeros((2, 3, LANES, LANES), F32)
    beta0, a0 = NARROW_OFFSET['gdn_beta'][0], NARROW_OFFSET['gdn_a'][0]
    h = jnp.arange(GDN_HEADS)
    for d in range(2):
        sel = sel.at[d, 0, beta0 + GDN_HEADS * d + h, h].set(1.0)
        sel = sel.at[d, 1, a0 + GDN_HEADS * d + h, GDN_HEADS + h].set(1.0)
        sel = sel.at[d, 2, a0 + GDN_HEADS * d + h, 2 * GDN_HEADS + h].set(1.0)
    return sel


def gdn_features(narrow, a_log, dt_bias, *, tm=256):
    bsz, n_tok, _ = narrow.shape
    a0 = NARROW_OFFSET['gdn_a'][0]
    alog_row = jnp.zeros((1, LANES), F32).at[0, a0:a0 + 2 * GDN_HEADS].set(a_log.reshape(-1))
    dt_row = jnp.zeros((1, LANES), F32).at[0, a0:a0 + 2 * GDN_HEADS].set(dt_bias.reshape(-1))
    return pl.pallas_call(
        _gdn_feat_kernel,
        grid=(bsz, n_tok // tm),
        in_specs=[
            pl.BlockSpec((1, tm, LANES), lambda b, m: (b, m, 0)),
            pl.BlockSpec((1, LANES), lambda b, m: (0, 0)),
            pl.BlockSpec((1, LANES), lambda b, m: (0, 0)),
            pl.BlockSpec((2, 3, LANES, LANES), lambda b, m: (0, 0, 0, 0)),
        ],
        out_specs=pl.BlockSpec((2, 1, tm, LANES), lambda b, m: (0, b, m, 0)),
        out_shape=jax.ShapeDtypeStruct((2, bsz, n_tok, LANES), F32),
        compiler_params=_cparams("parallel", "parallel"),
        name="gdn_features",
    )(narrow, alog_row, dt_row, gdn_feature_select())


def _gdn_scan_kernel(q_ref, k_ref, v_ref, f_ref, gct_ref, s0_ref, o_ref, sfin_ref, s_scr, *, n_chunks):
    d = pl.program_id(1)
    i = pl.program_id(2)

    @pl.when(i == 0)
    def _():
        s_scr[...] = s0_ref[0, 0]

    incl, strict = _scan_masks(d)
    eye = (lax.broadcasted_iota(jnp.int32, (CHUNK, CHUNK), 0)
           == lax.broadcasted_iota(jnp.int32, (CHUNK, CHUNK), 1)).astype(F32)
    dot = functools.partial(jnp.dot, preferred_element_type=F32)

    def body(ci, carry):
        cc = jnp.where(d == 0, ci, n_chunks - 1 - ci)
        rows = pl.ds(pl.multiple_of(cc * CHUNK, CHUNK), CHUNK)
        feat = f_ref[0, 0, rows, :]
        gct = gct_ref[0, 0, cc]
        for h in range(GDN_HEADS):
            hs = slice(h * GDN_DK, (h + 1) * GDN_DK)
            beta = feat[:, h:h + 1]
            gc = feat[:, GDN_HEADS + h:GDN_HEADS + h + 1]
            gtot = feat[:, 2 * GDN_HEADS + h:2 * GDN_HEADS + h + 1]
            decay = jnp.where(incl, jnp.exp(jnp.where(incl, gc - gct[h:h + 1, :], 0.0)), 0.0)
            qh, kh, vh = q_ref[0, rows, hs], k_ref[0, rows, hs], v_ref[0, rows, hs]
            kf = kh.astype(F32)
            kk = lax.dot_general(kh, kh, NT_DIMS, preferred_element_type=F32)
            x = -jnp.where(strict, beta * kk * decay, 0.0)
            inv = eye + x
            power = x
            for _ in range(5):
                pb = power.astype(BF16)
                power = dot(pb, pb)
                inv = inv + dot(inv.astype(BF16), power.astype(BF16))
            inv = inv.astype(BF16)
            egc = jnp.exp(gc)
            u = dot(inv, (vh.astype(F32) * beta).astype(BF16))
            w = dot(inv, (kf * (beta * egc)).astype(BF16))
            k_dec = (kf * jnp.exp(gtot - gc)).astype(BF16)
            q_dec = (qh.astype(F32) * egc).astype(BF16)
            qk = jnp.where(incl, lax.dot_general(qh, kh, NT_DIMS, preferred_element_type=F32) * decay, 0.0)
            s = s_scr[h]
            sb = s.astype(BF16)
            v_new = u - dot(w.astype(BF16), sb)
            vb = v_new.astype(BF16)
            o_ref[0, 0, rows, hs] = dot(q_dec, sb) + dot(qk.astype(BF16), vb)
            s_scr[h] = s * jnp.exp(gtot[0:1, 0:1]) + lax.dot_general(k_dec, vb, TN_DIMS, preferred_element_type=F32)
        return carry

    lax.fori_loop(0, n_chunks, body, 0)

    @pl.when(i == pl.num_programs(2) - 1)
    def _():
        sfin_ref[0, 0] = s_scr[...]


def gdn_scan(qkv, feat, s0, *, tb):
    bsz, n_tok, _ = qkv.shape
    tb = min(tb, n_tok)
    n_blk = n_tok // tb
    n_chunks = tb // CHUNK
    gct = feat[..., GDN_HEADS:2 * GDN_HEADS].reshape(2, bsz, n_tok // CHUNK, CHUNK, GDN_HEADS).swapaxes(3, 4)

    def blk(d, i):
        return jnp.where(d == 0, i, n_blk - 1 - i)

    state_spec = pl.BlockSpec((1, 1, GDN_HEADS, GDN_DK, GDN_DV), lambda b, d, i: (b, d, 0, 0, 0))
    return pl.pallas_call(
        functools.partial(_gdn_scan_kernel, n_chunks=n_chunks),
        grid=(bsz, 2, n_blk),
        in_specs=[
            pl.BlockSpec((1, tb, GDN_KW), lambda b, d, i: (b, blk(d, i), 0)),
            pl.BlockSpec((1, tb, GDN_KW), lambda b, d, i: (b, blk(d, i), 1)),
            pl.BlockSpec((1, tb, GDN_VW), lambda b, d, i: (b, blk(d, i), 2)),
            pl.BlockSpec((1, 1, tb, LANES), lambda b, d, i: (d, b, blk(d, i), 0)),
            pl.BlockSpec((1, 1, n_chunks, GDN_HEADS, CHUNK), lambda b, d, i: (d, b, blk(d, i), 0, 0)),
            state_spec,
        ],
        out_specs=[
            pl.BlockSpec((1, 1, tb, GDN_VW), lambda b, d, i: (d, b, blk(d, i), 0)),
            state_spec,
        ],
        out_shape=[
            jax.ShapeDtypeStruct((2, bsz, n_tok, GDN_VW), F32),
            jax.ShapeDtypeStruct((bsz, 2, GDN_HEADS, GDN_DK, GDN_DV), F32),
        ],
        scratch_shapes=[pltpu.VMEM((GDN_HEADS, GDN_DK, GDN_DV), F32)],
        compiler_params=_cparams("parallel", "parallel", "arbitrary"),
        name="gdn_scan",
    )(qkv, qkv, qkv, feat, gct, s0)


def _gated_norm_kernel(of_ref, ob_ref, z_ref, w_ref, y_ref, *, head_dim):
    width = z_ref.shape[2]
    for h in range(width // head_dim):
        hs = slice(h * head_dim, (h + 1) * head_dim)
        o = of_ref[0, 0, :, hs] + ob_ref[0, 0, :, hs]
        z = z_ref[0, :, hs].astype(F32)
        y = o * lax.rsqrt(jnp.mean(o * o, axis=-1, keepdims=True) + EPS) * w_ref[...]
        y_ref[0, :, hs] = (y * (z * jax.nn.sigmoid(z))).astype(y_ref.dtype)


def gated_head_norm(o2, proj, z_name, w, *, head_dim, tm=512):
    _, bsz, n_tok, width = o2.shape
    tm = min(tm, n_tok)
    z_blk = WIDE_OFFSET[z_name][0] // width
    return pl.pallas_call(
        functools.partial(_gated_norm_kernel, head_dim=head_dim),
        grid=(bsz, n_tok // tm),
        in_specs=[
            pl.BlockSpec((1, 1, tm, width), lambda b, m: (0, b, m, 0)),
            pl.BlockSpec((1, 1, tm, width), lambda b, m: (1, b, m, 0)),
            pl.BlockSpec((1, tm, width), lambda b, m: (b, m, z_blk)),
            pl.BlockSpec((1, head_dim), lambda b, m: (0, 0)),
        ],
        out_specs=pl.BlockSpec((1, tm, width), lambda b, m: (b, m, 0)),
        out_shape=jax.ShapeDtypeStruct((bsz, n_tok, width), BF16),
        compiler_params=_cparams("parallel", "parallel"),
        name="gated_head_norm",
    )(o2, o2, proj, w.reshape(1, head_dim))


def gdn_branch(proj_l, narrow_l, proj_c, narrow_c, conv_w, a_log, dt_bias, norm_w):
    bsz = proj_l.shape[0]
    conv_w8 = jnp.pad(conv_w.astype(F32), ((0, 8 - CONV_W), (0, 0)))
    s0 = jnp.zeros((bsz, 2, GDN_HEADS, GDN_DK, GDN_DV), F32)
    o_c, s_c = gdn_scan(gdn_conv_act(proj_c, conv_w8, tm=512), gdn_features(narrow_c, a_log, dt_bias), s0, tb=SCAN_TB)
    o_l, _ = gdn_scan(gdn_conv_act(proj_l, conv_w8, tm=512), gdn_features(narrow_l, a_log, dt_bias), s_c, tb=SCAN_TB)
    y = gated_head_norm(o_l, proj_l, 'gdn_z', norm_w, head_dim=GDN_DV)
    yc = gated_head_norm(o_c, proj_c, 'gdn_z', norm_w, head_dim=GDN_DV)
    return y, yc


def _gla_scan_kernel(q_ref, k_ref, v_ref, lr_ref, w2_ref, b2_ref, cos_ref, sin_ref, s0_ref, o_ref, sfin_ref, s_scr, *,
                     n_chunks):
    d = pl.program_id(1)
    i = pl.program_id(2)

    @pl.when(i == 0)
    def _():
        s_scr[...] = s0_ref[0, 0]

    incl, _ = _scan_masks(d)
    tri = incl.astype(F32)
    lane = lax.broadcasted_iota(jnp.int32, (CHUNK, GLA_DK), 1)
    first_half = (lane % (GLA_DK // 2)) < (GLA_DK // 4)
    dot = functools.partial(jnp.dot, preferred_element_type=F32)

    def body(ci, carry):
        cc = jnp.where(d == 0, ci, n_chunks - 1 - ci)
        rows = pl.ds(pl.multiple_of(cc * CHUNK, CHUNK), CHUNK)
        pre = dot(lr_ref[0, rows, :].astype(BF16), w2_ref[0].astype(BF16)) + b2_ref[0]
        log_a = (jnp.minimum(pre, 0.0) - jnp.log(1.0 + jnp.exp(-jnp.abs(pre)))) * (1.0 / GLA_GATE_NORM)
        cum = dot(tri, log_a, precision=HIGHEST)
        total = jnp.sum(log_a, axis=0, keepdims=True)
        cos, sin = cos_ref[rows, :], sin_ref[rows, :]

        def rope(t):
            partner = jnp.where(first_half, pltpu.roll(t, GLA_DK - GLA_DK // 4, 1), pltpu.roll(t, GLA_DK // 4, 1))
            return t * cos + partner * sin

        for h in range(GLA_HEADS):
            ks = slice(h * GLA_DK, (h + 1) * GLA_DK)
            vs = slice(h * GLA_DV, (h + 1) * GLA_DV)
            b, bt = cum[:, ks], total[:, ks]
            qh = rope(q_ref[0, rows, ks].astype(F32)) * GLA_DK ** -0.5
            kh = rope(k_ref[0, rows, ks].astype(F32))
            vh = v_ref[0, rows, vs]
            q_dec = (qh * jnp.exp(b)).astype(BF16)
            k_inv = (kh * jnp.exp(-b)).astype(BF16)
            k_dec = (kh * jnp.exp(bt - b)).astype(BF16)
            scores = jnp.where(incl, lax.dot_general(q_dec, k_inv, NT_DIMS, preferred_element_type=F32), 0.0)
            st = s_scr[h]
            o_ref[0, 0, rows, vs] = dot(scores.astype(BF16), vh) + lax.dot_general(
                q_dec, st.astype(BF16), NT_DIMS, preferred_element_type=F32)
            s_scr[h] = st * jnp.exp(bt) + lax.dot_general(vh, k_dec, TN_DIMS, preferred_element_type=F32)
        return carry

    lax.fori_loop(0, n_chunks, body, 0)

    @pl.when(i == pl.num_programs(2) - 1)
    def _():
        sfin_ref[0, 0] = s_scr[...]


def gla_scan(proj, narrow, w2p, b2p, cos, sin, s0, *, tb):
    bsz, n_tok, _ = proj.shape
    tb = min(tb, n_tok)
    n_blk = n_tok // tb
    q_blk = WIDE_OFFSET['gla_q'][0] // GLA_KW
    k_blk = WIDE_OFFSET['gla_k'][0] // GLA_KW
    v_blk = WIDE_OFFSET['gla_v'][0] // GLA_VW

    def blk(d, i):
        return jnp.where(d == 0, i, n_blk - 1 - i)

    state_spec = pl.BlockSpec((1, 1, GLA_HEADS, GLA_DV, GLA_DK), lambda b, d, i: (b, d, 0, 0, 0))
    return pl.pallas_call(
        functools.partial(_gla_scan_kernel, n_chunks=tb // CHUNK),
        grid=(bsz, 2, n_blk),
        in_specs=[
            pl.BlockSpec((1, tb, GLA_KW), lambda b, d, i: (b, blk(d, i), q_blk)),
            pl.BlockSpec((1, tb, GLA_KW), lambda b, d, i: (b, blk(d, i), k_blk)),
            pl.BlockSpec((1, tb, GLA_VW), lambda b, d, i: (b, blk(d, i), v_blk)),
            pl.BlockSpec((1, tb, LANES), lambda b, d, i: (b, blk(d, i), 0)),
            pl.BlockSpec((1, LANES, GLA_KW), lambda b, d, i: (d, 0, 0)),
            pl.BlockSpec((1, 1, GLA_KW), lambda b, d, i: (d, 0, 0)),
            pl.BlockSpec((tb, GLA_DK), lambda b, d, i: (blk(d, i), 0)),
            pl.BlockSpec((tb, GLA_DK), lambda b, d, i: (blk(d, i), 0)),
            state_spec,
        ],
        out_specs=[
            pl.BlockSpec((1, 1, tb, GLA_VW), lambda b, d, i: (d, b, blk(d, i), 0)),
            state_spec,
        ],
        out_shape=[
            jax.ShapeDtypeStruct((2, bsz, n_tok, GLA_VW), F32),
            jax.ShapeDtypeStruct((bsz, 2, GLA_HEADS, GLA_DV, GLA_DK), F32),
        ],
        scratch_shapes=[pltpu.VMEM((GLA_HEADS, GLA_DV, GLA_DK), F32)],
        compiler_params=_cparams("parallel", "parallel", "arbitrary"),
        name="gla_scan",
    )(proj, proj, proj, narrow, w2p, b2p, cos, sin, s0)


def rope_tables(n_tokens):
    axis_dim = GLA_DK // 2
    inv_freq = ROPE_BASE ** (-jnp.arange(0, axis_dim, 2, dtype=F32) / axis_dim)
    pos = jnp.arange(n_tokens)
    ang_row = (pos // GRID_W).astype(F32)[:, None] * inv_freq
    ang_col = (pos % GRID_W).astype(F32)[:, None] * inv_freq
    cos = jnp.concatenate([jnp.cos(ang_row)] * 2 + [jnp.cos(ang_col)] * 2, axis=-1)
    sin = jnp.concatenate([-jnp.sin(ang_row), jnp.sin(ang_row), -jnp.sin(ang_col), jnp.sin(ang_col)], axis=-1)
    return cos, sin


def gla_branch(proj_l, narrow_l, proj_c, narrow_c, w2, b2, norm_w, rope_l):
    bsz = proj_l.shape[0]
    n_ctx = proj_c.shape[1]
    lr0 = NARROW_OFFSET['gla_lr'][0]
    w2p = jnp.zeros((2, LANES, GLA_KW), F32)
    for d in range(2):
        w2p = w2p.at[d, lr0 + GLA_LOWRANK * d:lr0 + GLA_LOWRANK * (d + 1), :].set(w2[d].astype(F32))
    b2p = b2.astype(F32).reshape(2, 1, GLA_KW)
    no_rope = (jnp.ones((n_ctx, GLA_DK), F32), jnp.zeros((n_ctx, GLA_DK), F32))
    s0 = jnp.zeros((bsz, 2, GLA_HEADS, GLA_DV, GLA_DK), F32)
    o_c, s_c = gla_scan(proj_c, narrow_c, w2p, b2p, *no_rope, s0, tb=SCAN_TB)
    o_l, _ = gla_scan(proj_l, narrow_l, w2p, b2p, *rope_l, s_c, tb=SCAN_TB)
    y = gated_head_norm(o_l, proj_l, 'gla_z', norm_w, head_dim=GLA_DV)
    yc = gated_head_norm(o_c, proj_c, 'gla_z', norm_w, head_dim=GLA_DV)
    return y, yc


def split_heads(t, n_heads):
    return t.reshape(t.shape[:-1] + (n_heads, t.shape[-1] // n_heads))


def merge_heads(t):
    return t.reshape(t.shape[:-2] + (t.shape[-2] * t.shape[-1],))


def rev(t):
    return None if t is None else jnp.flip(t, axis=1)


def l2_normalize(t):
    return t * lax.rsqrt(jnp.sum(t * t, axis=-1, keepdims=True) + EPS)


def centred_conv(t, w):
    pad = CONV_W // 2
    return lax.conv_general_dilated(
        t, w[:, None, :].astype(t.dtype), window_strides=(1,), padding=((pad, pad),),
        dimension_numbers=('NWC', 'WIO', 'NWC'), feature_group_count=t.shape[-1])


def axial_rope_angles(n_tokens, head_dim):
    axis_dim = head_dim // 2
    inv_freq = ROPE_BASE ** (-jnp.arange(0, axis_dim, 2, dtype=F32) / axis_dim)
    pos = jnp.arange(n_tokens)
    row = (pos // GRID_W).astype(F32)
    col = (pos % GRID_W).astype(F32)
    return row[:, None] * inv_freq, col[:, None] * inv_freq


def rope_rotate(t, ang):
    cos = jnp.cos(ang)[:, None, :]
    sin = jnp.sin(ang)[:, None, :]
    t1, t2 = jnp.split(t, 2, axis=-1)
    return jnp.concatenate([t1 * cos - t2 * sin, t2 * cos + t1 * sin], axis=-1)


def apply_rope2d(t, ang):
    ang_row, ang_col = ang
    tf = t.astype(F32)
    half = tf.shape[-1] // 2
    return jnp.concatenate([rope_rotate(tf[..., :half], ang_row), rope_rotate(tf[..., half:], ang_col)], axis=-1)


def gated_head_norm_ref(o, z, w):
    y = o * lax.rsqrt(jnp.mean(o * o, axis=-1, keepdims=True) + EPS) * w.astype(F32)
    return merge_heads(y) * jax.nn.silu(z.astype(F32))


def bidirectional_with_prefix(scan_fn, lat_fwd, lat_bwd, ctx_fwd, ctx_bwd, s0):
    oc_f, sc_f = scan_fn(*ctx_fwd, s0)
    ol_f, _ = scan_fn(*lat_fwd, sc_f)
    oc_b, sc_b = scan_fn(*[rev(t) for t in ctx_bwd], s0)
    ol_b, _ = scan_fn(*[rev(t) for t in lat_bwd], sc_b)
    o_lat = ol_f + rev(ol_b)
    o_ctx = None if oc_f is None else oc_f + rev(oc_b)
    return o_lat, o_ctx


def chunk_gated_delta(q, k, v, g, beta, s0):
    bsz, n_tok, n_heads, _ = k.shape
    n = n_tok // CHUNK
    to_chunks = lambda t: t.reshape(bsz, n, CHUNK, n_heads, t.shape[-1]).transpose(1, 0, 3, 2, 4)
    kc, vc = to_chunks(k), to_chunks(v)
    gc = jnp.cumsum(g.reshape(bsz, n, CHUNK, n_heads).transpose(1, 0, 3, 2), axis=-1)
    bc = beta.reshape(bsz, n, CHUNK, n_heads).transpose(1, 0, 3, 2)
    causal = jnp.tril(jnp.ones((CHUNK, CHUNK), bool))
    strict = jnp.tril(jnp.ones((CHUNK, CHUNK), bool), -1)
    decay = jnp.exp(jnp.where(causal, gc[..., :, None] - gc[..., None, :], -jnp.inf))
    kb = kc * bc[..., None]
    lower = jnp.where(strict, jnp.einsum('nbhik,nbhjk->nbhij', kb, kc) * decay, 0.0)
    unit_lower = lower + jnp.eye(CHUNK, dtype=lower.dtype)
    u = lax.linalg.triangular_solve(unit_lower, vc * bc[..., None], left_side=True, lower=True, unit_diagonal=True)
    w = lax.linalg.triangular_solve(unit_lower, kb * jnp.exp(gc)[..., None], left_side=True, lower=True,
                                    unit_diagonal=True)
    g_last = gc[..., -1]
    k_dec = kc * jnp.exp(g_last[..., None] - gc)[..., None]

    def new_values(s, u_i, w_i):
        return u_i - jnp.einsum('bhck,bhkv->bhcv', w_i, s)

    def update(s, v_new, kd_i, gl_i):
        return s * jnp.exp(gl_i)[..., None, None] + jnp.einsum('bhck,bhcv->bhkv', kd_i, v_new)

    if q is None:
        def state_step(s, xs):
            u_i, w_i, kd_i, gl_i = xs
            return update(s, new_values(s, u_i, w_i), kd_i, gl_i), None
        s_fin, _ = lax.scan(state_step, s0, (u, w, k_dec, g_last))
        return None, s_fin

    qc = to_chunks(q)
    q_dec = qc * jnp.exp(gc)[..., None]
    qk = jnp.where(causal, jnp.einsum('nbhik,nbhjk->nbhij', qc, kc) * decay, 0.0)

    def step(s, xs):
        u_i, w_i, kd_i, gl_i, qd_i, qk_i = xs
        v_new = new_values(s, u_i, w_i)
        o = jnp.einsum('bhck,bhkv->bhcv', qd_i, s) + jnp.einsum('bhij,bhjv->bhiv', qk_i, v_new)
        return update(s, v_new, kd_i, gl_i), o

    s_fin, o = lax.scan(step, s0, (u, w, k_dec, g_last, q_dec, qk))
    return o.transpose(1, 0, 3, 2, 4).reshape(bsz, n_tok, n_heads, -1), s_fin


def chunk_gla(q, k, v, log_a, s0):
    bsz, n_tok, n_heads, _ = k.shape
    n = n_tok // CHUNK
    to_chunks = lambda t: t.reshape(bsz, n, CHUNK, n_heads, t.shape[-1]).transpose(1, 0, 3, 2, 4)
    kc, vc = to_chunks(k), to_chunks(v)
    b = jnp.cumsum(to_chunks(log_a), axis=-2)
    b_last = b[..., -1:, :]
    k_dec = kc * jnp.exp(b_last - b)
    d_last = jnp.exp(b_last[..., 0, :])

    def update(s, k_i, v_i, d_i):
        return d_i[..., None] * s + jnp.einsum('bhck,bhcv->bhkv', k_i, v_i)

    if q is None:
        s_fin, _ = lax.scan(lambda s, xs: (update(s, *xs), None), s0, (k_dec, vc, d_last))
        return None, s_fin

    q_dec = to_chunks(q) * jnp.exp(b)
    k_inv = kc * jnp.exp(-b)
    causal = jnp.tril(jnp.ones((CHUNK, CHUNK), bool))
    scores = jnp.where(causal, jnp.einsum('nbhik,nbhjk->nbhij', q_dec, k_inv), 0.0)
    o_intra = jnp.einsum('nbhij,nbhjv->nbhiv', scores, vc)

    def step(s, xs):
        q_i, k_i, v_i, d_i = xs
        return update(s, k_i, v_i, d_i), jnp.einsum('bhck,bhkv->bhcv', q_i, s)

    s_fin, o_inter = lax.scan(step, s0, (q_dec, k_dec, vc, d_last))
    o = (o_intra + o_inter).transpose(1, 0, 3, 2, 4).reshape(bsz, n_tok, n_heads, -1)
    return o, s_fin


def gdn_prepare(pg, conv_w, a_log, dt_bias):
    bsz, n_tok, _ = pg['gdn_k'].shape
    conv_q, conv_k, conv_v = conv_w[:, :GDN_KW], conv_w[:, GDN_KW:2 * GDN_KW], conv_w[:, 2 * GDN_KW:]

    def conv_act(t, w):
        return split_heads(jax.nn.silu(centred_conv(t.astype(F32), w)), GDN_HEADS)

    k = l2_normalize(conv_act(pg['gdn_k'], conv_k))
    v = conv_act(pg['gdn_v'], conv_v)
    q = l2_normalize(conv_act(pg['gdn_q'], conv_q)) * GDN_DK ** -0.5 if 'gdn_q' in pg else None
    beta = jax.nn.sigmoid(pg['gdn_beta'].astype(F32)).reshape(bsz, n_tok, 2, GDN_HEADS)
    g = -jnp.exp(a_log.astype(F32)) * jax.nn.softplus(
        pg['gdn_a'].astype(F32).reshape(bsz, n_tok, 2, GDN_HEADS) + dt_bias.astype(F32))
    return q, k, v, beta, g


def gdn_mixer(pl_, pc, conv_w, a_log, dt_bias, norm_w, ctx_out):
    ql, kl, vl, bl, gl = gdn_prepare(pl_, conv_w, a_log, dt_bias)
    qc, kc, vc, bc, gc = gdn_prepare(pc, conv_w, a_log, dt_bias)
    s0 = jnp.zeros((kl.shape[0], GDN_HEADS, GDN_DK, GDN_DV), F32)
    o_lat, o_ctx = bidirectional_with_prefix(
        chunk_gated_delta,
        (ql, kl, vl, gl[:, :, 0], bl[:, :, 0]), (ql, kl, vl, gl[:, :, 1], bl[:, :, 1]),
        (qc, kc, vc, gc[:, :, 0], bc[:, :, 0]), (qc, kc, vc, gc[:, :, 1], bc[:, :, 1]), s0)
    y = gated_head_norm_ref(o_lat, pl_['gdn_z'], norm_w)
    yc = gated_head_norm_ref(o_ctx, pc['gdn_z'], norm_w) if ctx_out else None
    return y, yc


def gla_prepare(pg, w2, b2, ang):
    bsz, n_tok, _ = pg['gla_k'].shape

    def qk_heads(t):
        t = split_heads(t.astype(F32), GLA_HEADS)
        if ang is not None:
            t = apply_rope2d(t, ang)
        return t

    k = qk_heads(pg['gla_k'])
    q = qk_heads(pg['gla_q']) * GLA_DK ** -0.5 if 'gla_q' in pg else None
    v = split_heads(pg['gla_v'], GLA_HEADS).astype(F32)
    lr = pg['gla_lr'].astype(F32).reshape(bsz, n_tok, 2, GLA_LOWRANK)
    log_a = jax.nn.log_sigmoid(jnp.einsum('btzr,zrk->btzk', lr, w2.astype(F32)) + b2.astype(F32)) / GLA_GATE_NORM
    return q, k, v, log_a.reshape(bsz, n_tok, 2, GLA_HEADS, GLA_DK)


def gla_mixer(pl_, pc, w2, b2, norm_w, ang, ctx_out):
    ql, kl, vl, al = gla_prepare(pl_, w2, b2, ang)
    qc, kc, vc, ac = gla_prepare(pc, w2, b2, None)
    s0 = jnp.zeros((kl.shape[0], GLA_HEADS, GLA_DK, GLA_DV), F32)
    o_lat, o_ctx = bidirectional_with_prefix(
        chunk_gla,
        (ql, kl, vl, al[:, :, 0]), (ql, kl, vl, al[:, :, 1]),
        (qc, kc, vc, ac[:, :, 0]), (qc, kc, vc, ac[:, :, 1]), s0)
    y = gated_head_norm_ref(o_lat, pl_['gla_z'], norm_w)
    yc = gated_head_norm_ref(o_ctx, pc['gla_z'], norm_w) if ctx_out else None
    return y, yc


def split_groups(proj, narrow, drop=()):
    out = {}
    for name in WIDE_GROUPS:
        if name not in drop:
            off, width = WIDE_OFFSET[name]
            out[name] = proj[..., off:off + width]
    for name in NARROW_GROUPS:
        off, width = NARROW_OFFSET[name]
        out[name] = narrow[..., off:off + width]
    return out


def expert_choice_ffn(h, logits, w_gate, w_up, w_down):
    bsz, n_tok, d = h.shape
    cap = max(1, CAPACITY_FACTOR * n_tok // N_EXPERTS)
    affinity = jax.nn.softmax(logits, axis=-1)
    gate, idx = lax.top_k(jnp.swapaxes(affinity, 1, 2), cap)
    xs = jax.vmap(lambda hb, ib: hb[ib])(h, idx)
    ys = expert_ffn(xs, gate[..., None], w_gate, w_up, w_down, tc=512)
    return jax.vmap(lambda yb, ib: jnp.zeros((n_tok, d), F32).at[ib.reshape(-1)].add(yb.reshape(-1, d)))(ys, idx)


def pack_in_weight(w_in):
    wide = jnp.concatenate([w_in[:, IN_OFFSET[n][0]:IN_OFFSET[n][0] + IN_OFFSET[n][1]] for n in WIDE_GROUPS], axis=1)
    narrow = jnp.concatenate([w_in[:, IN_OFFSET[n][0]:IN_OFFSET[n][0] + IN_OFFSET[n][1]] for n in NARROW_GROUPS],
                             axis=1)
    narrow = jnp.pad(narrow, ((0, 0), (0, NARROW_WIDTH - narrow.shape[1])))
    return wide.astype(BF16), narrow.astype(BF16)


def kernel(x, c, ctx, c_ctx, norm1_w, norm2_w, final_norm_w, w_ada, b_ada, w_in, na_rpb, gdn_conv,
           gdn_a_log, gdn_dt_bias, gdn_norm_w, gla_w2, gla_b2, gla_norm_w, w_branch, w_out,
           w_router, w_gate, w_up, w_down):
    bsz, n_lat, d = x.shape
    rows = n_lat // GRID_W
    ang = axial_rope_angles(n_lat, GLA_DK)
    cond_lat = jax.nn.silu(c)
    cond_ctx = jnp.broadcast_to(jax.nn.silu(c_ctx)[None], (bsz, d))
    for layer in range(DEPTH):
        ctx_out = layer < DEPTH - 1
        mod_lat = jnp.split((cond_lat @ w_ada[layer] + b_ada[layer])[:, None, :], 6, axis=-1)
        mod_ctx = jnp.split((cond_ctx @ w_ada[layer] + b_ada[layer])[:, None, :], 6, axis=-1)
        w_wide, w_narrow = pack_in_weight(w_in[layer])
        wb = w_branch[layer].astype(BF16)
        wo = w_out[layer].astype(BF16)
        wg, wu, wd = w_gate[layer].astype(BF16), w_up[layer].astype(BF16), w_down[layer].astype(BF16)
        wr = jnp.pad(w_router[layer], ((0, 0), (0, LANES - N_EXPERTS)))

        proj_l, narrow_l = norm_mod_project(x, norm1_w[layer], mod_lat[0], mod_lat[1], w_wide, w_narrow,
                                            tm=1024, tn=1024)
        proj_c, narrow_c = norm_mod_project(ctx, norm1_w[layer], mod_ctx[0], mod_ctx[1], w_wide, w_narrow,
                                            tm=256, tn=1024)
        ya = neighbourhood_attention(proj_l, proj_c, na_rpb[layer], rows)
        yca = context_attention(proj_c) if ctx_out else None
        pl_ = split_groups(proj_l, narrow_l)
        pc = split_groups(proj_c, narrow_c)
        if not ctx_out:
            pc.pop('gdn_q')
            pc.pop('gla_q')
        yb, ycb = gdn_mixer(pl_, pc, gdn_conv[layer], gdn_a_log[layer], gdn_dt_bias[layer], gdn_norm_w[layer],
                            ctx_out)
        yb = yb.astype(BF16)
        yg, ycg = gla_mixer(pl_, pc, gla_w2[layer], gla_b2[layer], gla_norm_w[layer], ang, ctx_out)
        yg = yg.astype(BF16)

        y = merge_branches(proj_l, ya, yb, yg, wb, tm=1024, tn=1024)
        x = out_proj_residual(y, wo, x, mod_lat[2], tm=1024, tn=1024)
        h2, logits = norm_mod_router(x, norm2_w[layer], mod_lat[3], mod_lat[4], wr, tm=512)
        x = x + mod_lat[5] * expert_choice_ffn(h2, logits[..., :N_EXPERTS], wg, wu, wd)
        if ctx_out:
            yc = merge_branches(proj_c, yca.astype(BF16), ycb.astype(BF16), ycg.astype(BF16), wb, tm=256, tn=1024)
            ctx = out_proj_residual(yc, wo, ctx, mod_ctx[2], tm=256, tn=1024)
            hc2, logits_c = norm_mod_router(ctx, norm2_w[layer], mod_ctx[3], mod_ctx[4], wr, tm=256)
            ctx = ctx + mod_ctx[5] * expert_choice_ffn(hc2, logits_c[..., :N_EXPERTS], wg, wu, wd)
    return final_norm(x, final_norm_w, tm=512)
```

```python
import functools

import jax
import jax.numpy as jnp
import numpy as np
from jax import lax
from jax.experimental import pallas as pl
from jax.experimental.pallas import tpu as pltpu

D_MODEL = 2048
DEPTH = 4
GRID_W = 64

NA_HEADS = 8
NA_DIM = 128
WIN_ROWS = 8
WIN_COLS = 16
Q_COL_BLOCK = 16
KEY_COL_BAND = 32

GDN_HEADS = 8
GDN_DK = 128
GDN_DV = 128
CONV_W = 5

GLA_HEADS = 4
GLA_DK = 128
GLA_DV = 256
GLA_LOWRANK = 16
GLA_GATE_NORM = 16.0

CHUNK = 64
ROPE_BASE = 10000.0

N_BRANCH = 3
BRANCH_W = 1024

N_EXPERTS = 16
EXPERT_FF = 1024
CAPACITY_FACTOR = 2

EPS = 1e-6
F32 = jnp.float32
BF16 = jnp.bfloat16

NA_W = NA_HEADS * NA_DIM
GDN_KW = GDN_HEADS * GDN_DK
GDN_VW = GDN_HEADS * GDN_DV
GLA_KW = GLA_HEADS * GLA_DK
GLA_VW = GLA_HEADS * GLA_DV

IN_GROUPS = (
    ('na_q', NA_W), ('na_k', NA_W), ('na_v', NA_W),
    ('gdn_q', GDN_KW), ('gdn_k', GDN_KW), ('gdn_v', GDN_VW), ('gdn_z', GDN_VW),
    ('gdn_beta', 2 * GDN_HEADS), ('gdn_a', 2 * GDN_HEADS),
    ('gla_q', GLA_KW), ('gla_k', GLA_KW), ('gla_v', GLA_VW), ('gla_z', GLA_VW),
    ('gla_lr', 2 * GLA_LOWRANK),
    ('gate_na', D_MODEL), ('gate_gdn', D_MODEL), ('gate_gla', D_MODEL),
)
IN_OFFSET = {}
_off = 0
for _name, _width in IN_GROUPS:
    IN_OFFSET[_name] = (_off, _width)
    _off += _width

WIDE_GROUPS = ('na_q', 'na_k', 'na_v', 'gdn_q', 'gdn_k', 'gdn_v', 'gdn_z',
               'gla_q', 'gla_k', 'gla_v', 'gla_z', 'gate_na', 'gate_gdn', 'gate_gla')
NARROW_GROUPS = ('gdn_beta', 'gdn_a', 'gla_lr')
WIDE_OFFSET = {}
_off = 0
for _name in WIDE_GROUPS:
    WIDE_OFFSET[_name] = (_off, IN_OFFSET[_name][1])
    _off += IN_OFFSET[_name][1]
WIDE_WIDTH = _off
NARROW_OFFSET = {}
_off = 0
for _name in NARROW_GROUPS:
    NARROW_OFFSET[_name] = (_off, IN_OFFSET[_name][1])
    _off += IN_OFFSET[_name][1]
LANES = 128
NARROW_WIDTH = LANES

VMEM_LIMIT = 56 * 1024 * 1024


def _cparams(*sem):
    return pltpu.CompilerParams(dimension_semantics=sem, vmem_limit_bytes=VMEM_LIMIT)


def _norm_mod_rows(x, nw, shift, scale):
    y = x * lax.rsqrt(jnp.mean(x * x, axis=-1, keepdims=True) + EPS) * nw
    return y * (1.0 + scale) + shift


def _proj_kernel(x_ref, nw_ref, sh_ref, sc_ref, w_ref, wn_ref, o_ref, on_ref, h_scr, *, row_chunk):
    n = pl.program_id(2)
    tm = x_ref.shape[1]

    @pl.when(n == 0)
    def _():
        def body(i, carry):
            rows = pl.ds(pl.multiple_of(i * row_chunk, row_chunk), row_chunk)
            h = _norm_mod_rows(x_ref[0, rows, :], nw_ref[...], sh_ref[0], sc_ref[0])
            h_scr[rows, :] = h.astype(BF16)
            return carry
        lax.fori_loop(0, tm // row_chunk, body, 0)
        on_ref[0] = jnp.dot(h_scr[...], wn_ref[...], preferred_element_type=F32)

    o_ref[0] = jnp.dot(h_scr[...], w_ref[...], preferred_element_type=F32).astype(o_ref.dtype)


def norm_mod_project(x, nw, shift, scale, w_wide, w_narrow, *, tm, tn):
    bsz, n_tok, d = x.shape
    n_wide = w_wide.shape[1]
    n_narrow = w_narrow.shape[1]
    tm = min(tm, n_tok)
    row_chunk = min(128, tm)
    grid = (bsz, n_tok // tm, n_wide // tn)
    return pl.pallas_call(
        functools.partial(_proj_kernel, row_chunk=row_chunk),
        grid=grid,
        in_specs=[
            pl.BlockSpec((1, tm, d), lambda b, m, n: (b, m, 0)),
            pl.BlockSpec((1, d), lambda b, m, n: (0, 0)),
            pl.BlockSpec((1, 1, d), lambda b, m, n: (b, 0, 0)),
            pl.BlockSpec((1, 1, d), lambda b, m, n: (b, 0, 0)),
            pl.BlockSpec((d, tn), lambda b, m, n: (0, n)),
            pl.BlockSpec((d, n_narrow), lambda b, m, n: (0, 0)),
        ],
        out_specs=[
            pl.BlockSpec((1, tm, tn), lambda b, m, n: (b, m, n)),
            pl.BlockSpec((1, tm, n_narrow), lambda b, m, n: (b, m, 0)),
        ],
        out_shape=[
            jax.ShapeDtypeStruct((bsz, n_tok, n_wide), BF16),
            jax.ShapeDtypeStruct((bsz, n_tok, n_narrow), F32),
        ],
        scratch_shapes=[pltpu.VMEM((tm, d), BF16)],
        compiler_params=_cparams("parallel", "parallel", "arbitrary"),
        name="norm_mod_project",
    )(x, nw.reshape(1, d), shift, scale, w_wide, w_narrow)


def _merge_kernel(ga_ref, gb_ref, gg_ref, ya_ref, yb_ref, yg_ref, wb_ref, o_ref):
    acc = jax.nn.sigmoid(ga_ref[0].astype(F32)) * jnp.dot(ya_ref[0], wb_ref[0], preferred_element_type=F32)
    acc += jax.nn.sigmoid(gb_ref[0].astype(F32)) * jnp.dot(yb_ref[0], wb_ref[1], preferred_element_type=F32)
    acc += jax.nn.sigmoid(gg_ref[0].astype(F32)) * jnp.dot(yg_ref[0], wb_ref[2], preferred_element_type=F32)
    o_ref[0] = acc.astype(o_ref.dtype)


def merge_branches(proj, ya, yb, yg, w_branch, *, tm, tn):
    bsz, n_tok, _ = proj.shape
    d = w_branch.shape[2]
    tm = min(tm, n_tok)
    gate_blk = [WIDE_OFFSET[name][0] // tn for name in ('gate_na', 'gate_gdn', 'gate_gla')]

    def gate_spec(blk):
        return pl.BlockSpec((1, tm, tn), lambda b, m, n: (b, m, blk + n))

    br_spec = pl.BlockSpec((1, tm, BRANCH_W), lambda b, m, n: (b, m, 0))
    return pl.pallas_call(
        _merge_kernel,
        grid=(bsz, n_tok // tm, d // tn),
        in_specs=[gate_spec(gate_blk[0]), gate_spec(gate_blk[1]), gate_spec(gate_blk[2]),
                  br_spec, br_spec, br_spec,
                  pl.BlockSpec((N_BRANCH, BRANCH_W, tn), lambda b, m, n: (0, 0, n))],
        out_specs=pl.BlockSpec((1, tm, tn), lambda b, m, n: (b, m, n)),
        out_shape=jax.ShapeDtypeStruct((bsz, n_tok, d), BF16),
        compiler_params=_cparams("parallel", "parallel", "arbitrary"),
        name="merge_branches",
    )(proj, proj, proj, ya, yb, yg, w_branch)


def _out_residual_kernel(y_ref, w_ref, x_ref, g_ref, o_ref):
    o_ref[0] = x_ref[0] + g_ref[0] * jnp.dot(y_ref[0], w_ref[...], preferred_element_type=F32)


def out_proj_residual(y, w_out, x, gate, *, tm, tn):
    bsz, n_tok, d = x.shape
    tm = min(tm, n_tok)
    return pl.pallas_call(
        _out_residual_kernel,
        grid=(bsz, n_tok // tm, d // tn),
        in_specs=[
            pl.BlockSpec((1, tm, d), lambda b, m, n: (b, m, 0)),
            pl.BlockSpec((d, tn), lambda b, m, n: (0, n)),
            pl.BlockSpec((1, tm, tn), lambda b, m, n: (b, m, n)),
            pl.BlockSpec((1, 1, tn), lambda b, m, n: (b, 0, n)),
        ],
        out_specs=pl.BlockSpec((1, tm, tn), lambda b, m, n: (b, m, n)),
        out_shape=jax.ShapeDtypeStruct((bsz, n_tok, d), F32),
        compiler_params=_cparams("parallel", "parallel", "arbitrary"),
        name="out_proj_residual",
    )(y, w_out, x, gate)


def _norm_router_kernel(x_ref, nw_ref, sh_ref, sc_ref, wr_ref, h_ref, l_ref):
    h = _norm_mod_rows(x_ref[0], nw_ref[...], sh_ref[0], sc_ref[0])
    h_ref[0] = h.astype(h_ref.dtype)
    l_ref[0] = jnp.dot(h, wr_ref[...], preferred_element_type=F32, precision=lax.Precision.HIGHEST)


def norm_mod_router(x, nw, shift, scale, w_router_pad, *, tm):
    bsz, n_tok, d = x.shape
    n_pad = w_router_pad.shape[1]
    tm = min(tm, n_tok)
    return pl.pallas_call(
        _norm_router_kernel,
        grid=(bsz, n_tok // tm),
        in_specs=[
            pl.BlockSpec((1, tm, d), lambda b, m: (b, m, 0)),
            pl.BlockSpec((1, d), lambda b, m: (0, 0)),
            pl.BlockSpec((1, 1, d), lambda b, m: (b, 0, 0)),
            pl.BlockSpec((1, 1, d), lambda b, m: (b, 0, 0)),
            pl.BlockSpec((d, n_pad), lambda b, m: (0, 0)),
        ],
        out_specs=[
            pl.BlockSpec((1, tm, d), lambda b, m: (b, m, 0)),
            pl.BlockSpec((1, tm, n_pad), lambda b, m: (b, m, 0)),
        ],
        out_shape=[
            jax.ShapeDtypeStruct((bsz, n_tok, d), BF16),
            jax.ShapeDtypeStruct((bsz, n_tok, n_pad), F32),
        ],
        compiler_params=_cparams("parallel", "parallel"),
        name="norm_mod_router",
    )(x, nw.reshape(1, d), shift, scale, w_router_pad)


def _expert_ffn_kernel(xs_ref, g_ref, wg_ref, wu_ref, wd_ref, o_ref):
    xs = xs_ref[0, 0]
    hid = jax.nn.silu(jnp.dot(xs, wg_ref[0], preferred_element_type=F32)) * jnp.dot(
        xs, wu_ref[0], preferred_element_type=F32)
    ys = jnp.dot(hid.astype(BF16), wd_ref[0], preferred_element_type=F32)
    o_ref[0, 0] = ys * g_ref[0, 0]


def expert_ffn(xs, gate, w_gate, w_up, w_down, *, tc):
    bsz, n_exp, cap, d = xs.shape
    ff = w_gate.shape[2]
    tc = min(tc, cap)
    return pl.pallas_call(
        _expert_ffn_kernel,
        grid=(n_exp, bsz, cap // tc),
        in_specs=[
            pl.BlockSpec((1, 1, tc, d), lambda e, b, c: (b, e, c, 0)),
            pl.BlockSpec((1, 1, tc, 1), lambda e, b, c: (b, e, c, 0)),
            pl.BlockSpec((1, d, ff), lambda e, b, c: (e, 0, 0)),
            pl.BlockSpec((1, d, ff), lambda e, b, c: (e, 0, 0)),
            pl.BlockSpec((1, ff, d), lambda e, b, c: (e, 0, 0)),
        ],
        out_specs=pl.BlockSpec((1, 1, tc, d), lambda e, b, c: (b, e, c, 0)),
        out_shape=jax.ShapeDtypeStruct((bsz, n_exp, cap, d), F32),
        compiler_params=_cparams("parallel", "parallel", "arbitrary"),
        name="expert_ffn",
    )(xs, gate, w_gate, w_up, w_down)


def _final_norm_kernel(x_ref, w_ref, o_ref):
    x = x_ref[0]
    o_ref[0] = x * lax.rsqrt(jnp.mean(x * x, axis=-1, keepdims=True) + EPS) * w_ref[...]


def final_norm(x, w, *, tm):
    bsz, n_tok, d = x.shape
    return pl.pallas_call(
        _final_norm_kernel,
        grid=(bsz, n_tok // tm),
        in_specs=[pl.BlockSpec((1, tm, d), lambda b, m: (b, m, 0)),
                  pl.BlockSpec((1, d), lambda b, m: (0, 0))],
        out_specs=pl.BlockSpec((1, tm, d), lambda b, m: (b, m, 0)),
        out_shape=jax.ShapeDtypeStruct((bsz, n_tok, d), F32),
        compiler_params=_cparams("parallel", "parallel"),
        name="final_norm",
    )(x, w.reshape(1, d))


NA_Q_ROWS = 8
NA_KEY_SLOTS = 4
NA_KV_ROWS = 4
NA_Q_TOK = NA_Q_ROWS * GRID_W
NA_KV_TOK = NA_KV_ROWS * GRID_W
NA_Q_CHUNK = 128


def na_bias_tables(rpb, rows):
    n_steps = rows // NA_Q_ROWS
    n_key_rows = NA_KEY_SLOTS * NA_KV_ROWS
    c = np.arange(GRID_W)
    ws = np.clip(c - WIN_COLS // 2, 0, GRID_W - WIN_COLS)
    col_ok = (c[None, :] >= ws[:, None]) & (c[None, :] < ws[:, None] + WIN_COLS)
    dc = np.clip(c[None, :] - c[:, None] + WIN_COLS - 1, 0, 2 * WIN_COLS - 2)
    pick_c = (np.arange(2 * WIN_COLS - 1) == dc[..., None]).astype(np.float32)
    tables = []
    for g in (0, 1, n_steps - 1):
        r = NA_Q_ROWS * g + np.arange(NA_Q_ROWS)
        kr = NA_Q_ROWS * g - WIN_ROWS // 2 + np.arange(n_key_rows)
        rs = np.clip(r - WIN_ROWS // 2, 0, rows - WIN_ROWS)
        row_ok = (kr[None, :] >= rs[:, None]) & (kr[None, :] < rs[:, None] + WIN_ROWS)
        dr = np.clip(kr[None, :] - r[:, None] + WIN_ROWS - 1, 0, 2 * WIN_ROWS - 2)
        pick_r = (np.arange(2 * WIN_ROWS - 1) == dr[..., None]).astype(np.float32)
        bias = jnp.einsum('abi,hij,cdj->hacbd', pick_r, rpb.astype(F32), pick_c, precision=HIGHEST)
        valid = row_ok[:, None, :, None] & col_ok[None, :, None, :]
        table = jnp.where(valid[None], bias, -jnp.inf)
        tables.append(table.reshape(rpb.shape[0], NA_Q_TOK, n_key_rows * GRID_W))
    return jnp.stack(tables)


def _na_kernel(q_ref, k0, k1, k2, k3, v0, v1, v2, v3, kc_ref, vc_ref, tbl_ref, o_ref):
    scale = NA_DIM ** -0.5
    k_refs = (k0, k1, k2, k3, kc_ref)
    v_refs = (v0, v1, v2, v3, vc_ref)
    contract_last = (((1,), (1,)), ((), ()))

    def body(i, carry):
        rows = pl.ds(pl.multiple_of(i * NA_Q_CHUNK, NA_Q_CHUNK), NA_Q_CHUNK)
        q = q_ref[0, rows, :]
        scores = []
        for s, k_ref in enumerate(k_refs):
            sc = lax.dot_general(q, k_ref[0], contract_last, preferred_element_type=F32) * scale
            if s < NA_KEY_SLOTS:
                sc = sc + tbl_ref[0, 0, rows, s * NA_KV_TOK:(s + 1) * NA_KV_TOK]
            scores.append(sc)
        m = scores[0].max(axis=-1, keepdims=True)
        for sc in scores[1:]:
            m = jnp.maximum(m, sc.max(axis=-1, keepdims=True))
        probs = [jnp.exp(sc - m) for sc in scores]
        denom = probs[0].sum(axis=-1, keepdims=True)
        for p in probs[1:]:
            denom = denom + p.sum(axis=-1, keepdims=True)
        inv = 1.0 / denom
        acc = jnp.zeros((NA_Q_CHUNK, NA_DIM), F32)
        for p, v_ref in zip(probs, v_refs):
            acc = acc + jnp.dot((p * inv).astype(BF16), v_ref[0], preferred_element_type=F32)
        o_ref[0, rows, :] = acc.astype(o_ref.dtype)
        return carry

    lax.fori_loop(0, NA_Q_TOK // NA_Q_CHUNK, body, 0)


def neighbourhood_attention(proj_l, proj_c, rpb, rows):
    bsz, n_tok, _ = proj_l.shape
    n_ctx = proj_c.shape[1]
    n_steps = rows // NA_Q_ROWS
    n_kv_blocks = n_tok // NA_KV_TOK
    assert n_steps >= 2 and rows % NA_Q_ROWS == 0
    tables = na_bias_tables(rpb, rows)
    q_blk, k_blk, v_blk = (WIDE_OFFSET[n][0] // NA_DIM for n in ('na_q', 'na_k', 'na_v'))

    def kv_spec(col_blk, slot):
        def idx(h, b, g):
            blk = jnp.clip(2 * g - 1 + slot, 0, n_kv_blocks - 1)
            return (b, blk, col_blk + h)
        return pl.BlockSpec((1, NA_KV_TOK, NA_DIM), idx)

    def tbl_idx(h, b, g):
        kind = jnp.where(g == 0, 0, jnp.where(g == n_steps - 1, 2, 1))
        return (kind, h, 0, 0)

    return pl.pallas_call(
        _na_kernel,
        grid=(NA_HEADS, bsz, n_steps),
        in_specs=[pl.BlockSpec((1, NA_Q_TOK, NA_DIM), lambda h, b, g: (b, g, q_blk + h))]
        + [kv_spec(k_blk, s) for s in range(NA_KEY_SLOTS)]
        + [kv_spec(v_blk, s) for s in range(NA_KEY_SLOTS)]
        + [pl.BlockSpec((1, n_ctx, NA_DIM), lambda h, b, g: (b, 0, k_blk + h)),
           pl.BlockSpec((1, n_ctx, NA_DIM), lambda h, b, g: (b, 0, v_blk + h)),
           pl.BlockSpec((1, 1, NA_Q_TOK, NA_KEY_SLOTS * NA_KV_TOK), tbl_idx)],
        out_specs=pl.BlockSpec((1, NA_Q_TOK, NA_DIM), lambda h, b, g: (b, g, h)),
        out_shape=jax.ShapeDtypeStruct((bsz, n_tok, NA_W), BF16),
        compiler_params=_cparams("parallel", "parallel", "arbitrary"),
        name="neighbourhood_attention",
    )(proj_l, *([proj_l] * (2 * NA_KEY_SLOTS)), proj_c, proj_c, tables)


def _ctx_attn_kernel(q_ref, k_ref, v_ref, o_ref):
    s = lax.dot_general(q_ref[0], k_ref[0], (((1,), (1,)), ((), ())), preferred_element_type=F32) * NA_DIM ** -0.5
    p = jnp.exp(s - s.max(axis=-1, keepdims=True))
    p = p / p.sum(axis=-1, keepdims=True)
    o_ref[0] = jnp.dot(p.astype(BF16), v_ref[0], preferred_element_type=F32).astype(o_ref.dtype)


def context_attention(proj_c):
    bsz, n_ctx, _ = proj_c.shape
    q_blk, k_blk, v_blk = (WIDE_OFFSET[n][0] // NA_DIM for n in ('na_q', 'na_k', 'na_v'))

    def spec(col_blk):
        return pl.BlockSpec((1, n_ctx, NA_DIM), lambda b, h: (b, 0, col_blk + h))

    return pl.pallas_call(
        _ctx_attn_kernel,
        grid=(bsz, NA_HEADS),
        in_specs=[spec(q_blk), spec(k_blk), spec(v_blk)],
        out_specs=pl.BlockSpec((1, n_ctx, NA_DIM), lambda b, h: (b, 0, h)),
        out_shape=jax.ShapeDtypeStruct((bsz, n_ctx, NA_W), BF16),
        compiler_params=_cparams("parallel", "parallel"),
        name="context_attention",
    )(proj_c, proj_c, proj_c)


HIGHEST = lax.Precision.HIGHEST
NT_DIMS = (((1,), (1,)), ((), ()))
TN_DIMS = (((0,), (0,)), ((), ()))
SCAN_TB = 512


def _scan_masks(d):
    r = lax.broadcasted_iota(jnp.int32, (CHUNK, CHUNK), 0)
    c = lax.broadcasted_iota(jnp.int32, (CHUNK, CHUNK), 1)
    diff = (r - c) * jnp.where(d == 0, 1, -1)
    return diff >= 0, diff > 0


def _softplus(x):
    return jnp.maximum(x, 0.0) + jnp.log(1.0 + jnp.exp(-jnp.abs(x)))


def _gdn_conv_kernel(x_ref, prev_ref, next_ref, w_ref, o_ref, ext_scr):
    m = pl.program_id(1)
    s = pl.program_id(2)
    n_m = pl.num_programs(1)
    tm = x_ref.shape[1]
    halo = 8
    pad = CONV_W // 2
    prev = prev_ref[0].astype(F32)[halo:, :]
    nxt = next_ref[0].astype(F32)[:halo, :]
    ext_scr[0:halo, :] = jnp.where(m > 0, prev, 0.0)
    ext_scr[halo:halo + tm, :] = x_ref[0].astype(F32)
    ext_scr[halo + tm:2 * halo + tm, :] = jnp.where(m < n_m - 1, nxt, 0.0)
    q_scale = jnp.where(s == 0, GDN_DK ** -0.5, 1.0)
    for h in range(GDN_HEADS):
        hs = slice(h * GDN_DK, (h + 1) * GDN_DK)
        acc = jnp.zeros((tm, GDN_DK), F32)
        for j in range(CONV_W):
            acc = acc + ext_scr[pl.ds(halo - pad + j, tm), hs] * w_ref[j:j + 1, hs]
        y = acc * jax.nn.sigmoid(acc)
        normed = y * lax.rsqrt(jnp.sum(y * y, axis=-1, keepdims=True) + EPS) * q_scale
        o_ref[0, :, hs] = jnp.where(s < 2, normed, y).astype(o_ref.dtype)


def gdn_conv_act(proj, conv_w8, *, tm):
    bsz, n_tok, _ = proj.shape
    tm = min(tm, n_tok)
    halo_rows = 16
    col0 = WIDE_OFFSET['gdn_q'][0] // GDN_KW
    n_halo = n_tok // halo_rows
    per = tm // halo_rows
    return pl.pallas_call(
        _gdn_conv_kernel,
        grid=(bsz, n_tok // tm, 3),
        in_specs=[
            pl.BlockSpec((1, tm, GDN_KW), lambda b, m, s: (b, m, col0 + s)),
            pl.BlockSpec((1, halo_rows, GDN_KW), lambda b, m, s: (b, jnp.maximum(m * per - 1, 0), col0 + s)),
            pl.BlockSpec((1, halo_rows, GDN_KW), lambda b, m, s: (b, jnp.minimum((m + 1) * per, n_halo - 1), col0 + s)),
            pl.BlockSpec((8, GDN_KW), lambda b, m, s: (0, s)),
        ],
        out_specs=pl.BlockSpec((1, tm, GDN_KW), lambda b, m, s: (b, m, s)),
        out_shape=jax.ShapeDtypeStruct((bsz, n_tok, 3 * GDN_KW), BF16),
        scratch_shapes=[pltpu.VMEM((tm + 16, GDN_KW), F32)],
        compiler_params=_cparams("parallel", "parallel", "arbitrary"),
        name="gdn_conv",
    )(proj, proj, proj, conv_w8)


def _gdn_feat_kernel(n_ref, alog_ref, dt_ref, sel_ref, f_ref):
    x = n_ref[0]
    tm = x.shape[0]
    beta = jax.nn.sigmoid(x)
    g = -jnp.exp(alog_ref[...]) * _softplus(x + dt_ref[...])
    r = lax.broadcasted_iota(jnp.int32, (tm, tm), 0)
    c = lax.broadcasted_iota(jnp.int32, (tm, tm), 1)
    same = (r // CHUNK) == (c // CHUNK)
    dot = functools.partial(jnp.dot, precision=HIGHEST, preferred_element_type=F32)
    prefix = dot((same & (r >= c)).astype(F32), g)
    suffix = dot((same & (r <= c)).astype(F32), g)
    total = dot(same.astype(F32), g)
    for d, cum in enumerate((prefix, suffix)):
        f_ref[d, 0] = dot(beta, sel_ref[d, 0]) + dot(cum, sel_ref[d, 1]) + dot(total, sel_ref[d, 2])


def gdn_feature_select():
    sel = jnp.zeros((2, 3, LANES, LANES), F32)
    beta0, a0 = NARROW_OFFSET['gdn_beta'][0], NARROW_OFFSET['gdn_a'][0]
    h = jnp.arange(GDN_HEADS)
    for d in range(2):
        sel = sel.at[d, 0, beta0 + GDN_HEADS * d + h, h].set(1.0)
        sel = sel.at[d, 1, a0 + GDN_HEADS * d + h, GDN_HEADS + h].set(1.0)
        sel = sel.at[d, 2, a0 + GDN_HEADS * d + h, 2 * GDN_HEADS + h].set(1.0)
    return sel


def gdn_features(narrow, a_log, dt_bias, *, tm=256):
    bsz, n_tok, _ = narrow.shape
    a0 = NARROW_OFFSET['gdn_a'][0]
    alog_row = jnp.zeros((1, LANES), F32).at[0, a0:a0 + 2 * GDN_HEADS].set(a_log.reshape(-1))
    dt_row = jnp.zeros((1, LANES), F32).at[0, a0:a0 + 2 * GDN_HEADS].set(dt_bias.reshape(-1))
    return pl.pallas_call(
        _gdn_feat_kernel,
        grid=(bsz, n_tok // tm),
        in_specs=[
            pl.BlockSpec((1, tm, LANES), lambda b, m: (b, m, 0)),
            pl.BlockSpec((1, LANES), lambda b, m: (0, 0)),
            pl.BlockSpec((1, LANES), lambda b, m: (0, 0)),
            pl.BlockSpec((2, 3, LANES, LANES), lambda b, m: (0, 0, 0, 0)),
        ],
        out_specs=pl.BlockSpec((2, 1, tm, LANES), lambda b, m: (0, b, m, 0)),
        out_shape=jax.ShapeDtypeStruct((2, bsz, n_tok, LANES), F32),
        compiler_params=_cparams("parallel", "parallel"),
        name="gdn_features",
    )(narrow, alog_row, dt_row, gdn_feature_select())


def _gdn_scan_kernel(q_ref, k_ref, v_ref, f_ref, gct_ref, s0_ref, o_ref, sfin_ref, s_scr, *, n_chunks):
    d = pl.program_id(1)
    i = pl.program_id(2)

    @pl.when(i == 0)
    def _():
        s_scr[...] = s0_ref[0, 0]

    incl, strict = _scan_masks(d)
    eye = (lax.broadcasted_iota(jnp.int32, (CHUNK, CHUNK), 0)
           == lax.broadcasted_iota(jnp.int32, (CHUNK, CHUNK), 1)).astype(F32)
    dot = functools.partial(jnp.dot, preferred_element_type=F32)

    def body(ci, carry):
        cc = jnp.where(d == 0, ci, n_chunks - 1 - ci)
        rows = pl.ds(pl.multiple_of(cc * CHUNK, CHUNK), CHUNK)
        feat = f_ref[0, 0, rows, :]
        gct = gct_ref[0, 0, cc]
        for h in range(GDN_HEADS):
            hs = slice(h * GDN_DK, (h + 1) * GDN_DK)
            beta = feat[:, h:h + 1]
            gc = feat[:, GDN_HEADS + h:GDN_HEADS + h + 1]
            gtot = feat[:, 2 * GDN_HEADS + h:2 * GDN_HEADS + h + 1]
            decay = jnp.where(incl, jnp.exp(jnp.where(incl, gc - gct[h:h + 1, :], 0.0)), 0.0)
            qh, kh, vh = q_ref[0, rows, hs], k_ref[0, rows, hs], v_ref[0, rows, hs]
            kf = kh.astype(F32)
            kk = lax.dot_general(kh, kh, NT_DIMS, preferred_element_type=F32)
            x = -jnp.where(strict, beta * kk * decay, 0.0)
            inv = eye + x
            power = x
            for _ in range(5):
                pb = power.astype(BF16)
                power = dot(pb, pb)
                inv = inv + dot(inv.astype(BF16), power.astype(BF16))
            inv = inv.astype(BF16)
            egc = jnp.exp(gc)
            u = dot(inv, (vh.astype(F32) * beta).astype(BF16))
            w = dot(inv, (kf * (beta * egc)).astype(BF16))
            k_dec = (kf * jnp.exp(gtot - gc)).astype(BF16)
            q_dec = (qh.astype(F32) * egc).astype(BF16)
            qk = jnp.where(incl, lax.dot_general(qh, kh, NT_DIMS, preferred_element_type=F32) * decay, 0.0)
            s = s_scr[h]
            sb = s.astype(BF16)
            v_new = u - dot(w.astype(BF16), sb)
            vb = v_new.astype(BF16)
            o_ref[0, 0, rows, hs] = dot(q_dec, sb) + dot(qk.astype(BF16), vb)
            s_scr[h] = s * jnp.exp(gtot[0:1, 0:1]) + lax.dot_general(k_dec, vb, TN_DIMS, preferred_element_type=F32)
        return carry

    lax.fori_loop(0, n_chunks, body, 0)

    @pl.when(i == pl.num_programs(2) - 1)
    def _():
        sfin_ref[0, 0] = s_scr[...]


def gdn_scan(qkv, feat, s0, *, tb):
    bsz, n_tok, _ = qkv.shape
    tb = min(tb, n_tok)
    n_blk = n_tok // tb
    n_chunks = tb // CHUNK
    gct = feat[..., GDN_HEADS:2 * GDN_HEADS].reshape(2, bsz, n_tok // CHUNK, CHUNK, GDN_HEADS).swapaxes(3, 4)

    def blk(d, i):
        return jnp.where(d == 0, i, n_blk - 1 - i)

    state_spec = pl.BlockSpec((1, 1, GDN_HEADS, GDN_DK, GDN_DV), lambda b, d, i: (b, d, 0, 0, 0))
    return pl.pallas_call(
        functools.partial(_gdn_scan_kernel, n_chunks=n_chunks),
        grid=(bsz, 2, n_blk),
        in_specs=[
            pl.BlockSpec((1, tb, GDN_KW), lambda b, d, i: (b, blk(d, i), 0)),
            pl.BlockSpec((1, tb, GDN_KW), lambda b, d, i: (b, blk(d, i), 1)),
            pl.BlockSpec((1, tb, GDN_VW), lambda b, d, i: (b, blk(d, i), 2)),
            pl.BlockSpec((1, 1, tb, LANES), lambda b, d, i: (d, b, blk(d, i), 0)),
            pl.BlockSpec((1, 1, n_chunks, GDN_HEADS, CHUNK), lambda b, d, i: (d, b, blk(d, i), 0, 0)),
            state_spec,
        ],
        out_specs=[
            pl.BlockSpec((1, 1, tb, GDN_VW), lambda b, d, i: (d, b, blk(d, i), 0)),
            state_spec,
        ],
        out_shape=[
            jax.ShapeDtypeStruct((2, bsz, n_tok, GDN_VW), F32),
            jax.ShapeDtypeStruct((bsz, 2, GDN_HEADS, GDN_DK, GDN_DV), F32),
        ],
        scratch_shapes=[pltpu.VMEM((GDN_HEADS, GDN_DK, GDN_DV), F32)],
        compiler_params=_cparams("parallel", "parallel", "arbitrary"),
        name="gdn_scan",
    )(qkv, qkv, qkv, feat, gct, s0)


def _gated_norm_kernel(of_ref, ob_ref, z_ref, w_ref, y_ref, *, head_dim):
    width = z_ref.shape[2]
    for h in range(width // head_dim):
        hs = slice(h * head_dim, (h + 1) * head_dim)
        o = of_ref[0, 0, :, hs] + ob_ref[0, 0, :, hs]
        z = z_ref[0, :, hs].astype(F32)
        y = o * lax.rsqrt(jnp.mean(o * o, axis=-1, keepdims=True) + EPS) * w_ref[...]
        y_ref[0, :, hs] = (y * (z * jax.nn.sigmoid(z))).astype(y_ref.dtype)


def gated_head_norm(o2, proj, z_name, w, *, head_dim, tm=512):
    _, bsz, n_tok, width = o2.shape
    tm = min(tm, n_tok)
    z_blk = WIDE_OFFSET[z_name][0] // width
    return pl.pallas_call(
        functools.partial(_gated_norm_kernel, head_dim=head_dim),
        grid=(bsz, n_tok // tm),
        in_specs=[
            pl.BlockSpec((1, 1, tm, width), lambda b, m: (0, b, m, 0)),
            pl.BlockSpec((1, 1, tm, width), lambda b, m: (1, b, m, 0)),
            pl.BlockSpec((1, tm, width), lambda b, m: (b, m, z_blk)),
            pl.BlockSpec((1, head_dim), lambda b, m: (0, 0)),
        ],
        out_specs=pl.BlockSpec((1, tm, width), lambda b, m: (b, m, 0)),
        out_shape=jax.ShapeDtypeStruct((bsz, n_tok, width), BF16),
        compiler_params=_cparams("parallel", "parallel"),
        name="gated_head_norm",
    )(o2, o2, proj, w.reshape(1, head_dim))


def gdn_branch(proj_l, narrow_l, proj_c, narrow_c, conv_w, a_log, dt_bias, norm_w):
    bsz = proj_l.shape[0]
    conv_w8 = jnp.pad(conv_w.astype(F32), ((0, 8 - CONV_W), (0, 0)))
    s0 = jnp.zeros((bsz, 2, GDN_HEADS, GDN_DK, GDN_DV), F32)
    o_c, s_c = gdn_scan(gdn_conv_act(proj_c, conv_w8, tm=512), gdn_features(narrow_c, a_log, dt_bias), s0, tb=SCAN_TB)
    o_l, _ = gdn_scan(gdn_conv_act(proj_l, conv_w8, tm=512), gdn_features(narrow_l, a_log, dt_bias), s_c, tb=SCAN_TB)
    y = gated_head_norm(o_l, proj_l, 'gdn_z', norm_w, head_dim=GDN_DV)
    yc = gated_head_norm(o_c, proj_c, 'gdn_z', norm_w, head_dim=GDN_DV)
    return y, yc


def _gla_scan_kernel(q_ref, k_ref, v_ref, lr_ref, w2_ref, b2_ref, cos_ref, sin_ref, s0_ref, o_ref, sfin_ref, s_scr, *,
                     n_chunks):
    d = pl.program_id(1)
    i = pl.program_id(2)

    @pl.when(i == 0)
    def _():
        s_scr[...] = s0_ref[0, 0]

    incl, _ = _scan_masks(d)
    tri = incl.astype(F32)
    lane = lax.broadcasted_iota(jnp.int32, (CHUNK, GLA_DK), 1)
    first_half = (lane % (GLA_DK // 2)) < (GLA_DK // 4)
    dot = functools.partial(jnp.dot, preferred_element_type=F32)

    def body(ci, carry):
        cc = jnp.where(d == 0, ci, n_chunks - 1 - ci)
        rows = pl.ds(pl.multiple_of(cc * CHUNK, CHUNK), CHUNK)
        pre = dot(lr_ref[0, rows, :].astype(BF16), w2_ref[0].astype(BF16)) + b2_ref[0]
        log_a = (jnp.minimum(pre, 0.0) - jnp.log(1.0 + jnp.exp(-jnp.abs(pre)))) * (1.0 / GLA_GATE_NORM)
        cum = dot(tri, log_a, precision=HIGHEST)
        total = jnp.sum(log_a, axis=0, keepdims=True)
        cos, sin = cos_ref[rows, :], sin_ref[rows, :]

        def rope(t):
            partner = jnp.where(first_half, pltpu.roll(t, GLA_DK - GLA_DK // 4, 1), pltpu.roll(t, GLA_DK // 4, 1))
            return t * cos + partner * sin

        for h in range(GLA_HEADS):
            ks = slice(h * GLA_DK, (h + 1) * GLA_DK)
            vs = slice(h * GLA_DV, (h + 1) * GLA_DV)
            b, bt = cum[:, ks], total[:, ks]
            qh = rope(q_ref[0, rows, ks].astype(F32)) * GLA_DK ** -0.5
            kh = rope(k_ref[0, rows, ks].astype(F32))
            vh = v_ref[0, rows, vs]
            q_dec = (qh * jnp.exp(b)).astype(BF16)
            k_inv = (kh * jnp.exp(-b)).astype(BF16)
            k_dec = (kh * jnp.exp(bt - b)).astype(BF16)
            scores = jnp.where(incl, lax.dot_general(q_dec, k_inv, NT_DIMS, preferred_element_type=F32), 0.0)
            st = s_scr[h]
            o_ref[0, 0, rows, vs] = dot(scores.astype(BF16), vh) + lax.dot_general(
                q_dec, st.astype(BF16), NT_DIMS, preferred_element_type=F32)
            s_scr[h] = st * jnp.exp(bt) + lax.dot_general(vh, k_dec, TN_DIMS, preferred_element_type=F32)
        return carry

    lax.fori_loop(0, n_chunks, body, 0)

    @pl.when(i == pl.num_programs(2) - 1)
    def _():
        sfin_ref[0, 0] = s_scr[...]


def gla_scan(proj, narrow, w2p, b2p, cos, sin, s0, *, tb):
    bsz, n_tok, _ = proj.shape
    tb = min(tb, n_tok)
    n_blk = n_tok // tb
    q_blk = WIDE_OFFSET['gla_q'][0] // GLA_KW
    k_blk = WIDE_OFFSET['gla_k'][0] // GLA_KW
    v_blk = WIDE_OFFSET['gla_v'][0] // GLA_VW

    def blk(d, i):
        return jnp.where(d == 0, i, n_blk - 1 - i)

    state_spec = pl.BlockSpec((1, 1, GLA_HEADS, GLA_DV, GLA_DK), lambda b, d, i: (b, d, 0, 0, 0))
    return pl.pallas_call(
        functools.partial(_gla_scan_kernel, n_chunks=tb // CHUNK),
        grid=(bsz, 2, n_blk),
        in_specs=[
            pl.BlockSpec((1, tb, GLA_KW), lambda b, d, i: (b, blk(d, i), q_blk)),
            pl.BlockSpec((1, tb, GLA_KW), lambda b, d, i: (b, blk(d, i), k_blk)),
            pl.BlockSpec((1, tb, GLA_VW), lambda b, d, i: (b, blk(d, i), v_blk)),
            pl.BlockSpec((1, tb, LANES), lambda b, d, i: (b, blk(d, i), 0)),
            pl.BlockSpec((1, LANES, GLA_KW), lambda b, d, i: (d, 0, 0)),
            pl.BlockSpec((1, 1, GLA_KW), lambda b, d, i: (d, 0, 0)),
            pl.BlockSpec((tb, GLA_DK), lambda b, d, i: (blk(d, i), 0)),
            pl.BlockSpec((tb, GLA_DK), lambda b, d, i: (blk(d, i), 0)),
            state_spec,
        ],
        out_specs=[
            pl.BlockSpec((1, 1, tb, GLA_VW), lambda b, d, i: (d, b, blk(d, i), 0)),
            state_spec,
        ],
        out_shape=[
            jax.ShapeDtypeStruct((2, bsz, n_tok, GLA_VW), F32),
            jax.ShapeDtypeStruct((bsz, 2, GLA_HEADS, GLA_DV, GLA_DK), F32),
        ],
        scratch_shapes=[pltpu.VMEM((GLA_HEADS, GLA_DV, GLA_DK), F32)],
        compiler_params=_cparams("parallel", "parallel", "arbitrary"),
        name="gla_scan",
    )(proj, proj, proj, narrow, w2p, b2p, cos, sin, s0)


def rope_tables(n_tokens):
    axis_dim = GLA_DK // 2
    inv_freq = ROPE_BASE ** (-jnp.arange(0, axis_dim, 2, dtype=F32) / axis_dim)
    pos = jnp.arange(n_tokens)
    ang_row = (pos // GRID_W).astype(F32)[:, None] * inv_freq
    ang_col = (pos % GRID_W).astype(F32)[:, None] * inv_freq
    cos = jnp.concatenate([jnp.cos(ang_row)] * 2 + [jnp.cos(ang_col)] * 2, axis=-1)
    sin = jnp.concatenate([-jnp.sin(ang_row), jnp.sin(ang_row), -jnp.sin(ang_col), jnp.sin(ang_col)], axis=-1)
    return cos, sin


def gla_branch(proj_l, narrow_l, proj_c, narrow_c, w2, b2, norm_w, rope_l):
    bsz = proj_l.shape[0]
    n_ctx = proj_c.shape[1]
    lr0 = NARROW_OFFSET['gla_lr'][0]
    w2p = jnp.zeros((2, LANES, GLA_KW), F32)
    for d in range(2):
        w2p = w2p.at[d, lr0 + GLA_LOWRANK * d:lr0 + GLA_LOWRANK * (d + 1), :].set(w2[d].astype(F32))
    b2p = b2.astype(F32).reshape(2, 1, GLA_KW)
    no_rope = (jnp.ones((n_ctx, GLA_DK), F32), jnp.zeros((n_ctx, GLA_DK), F32))
    s0 = jnp.zeros((bsz, 2, GLA_HEADS, GLA_DV, GLA_DK), F32)
    o_c, s_c = gla_scan(proj_c, narrow_c, w2p, b2p, *no_rope, s0, tb=SCAN_TB)
    o_l, _ = gla_scan(proj_l, narrow_l, w2p, b2p, *rope_l, s_c, tb=SCAN_TB)
    y = gated_head_norm(o_l, proj_l, 'gla_z', norm_w, head_dim=GLA_DV)
    yc = gated_head_norm(o_c, proj_c, 'gla_z', norm_w, head_dim=GLA_DV)
    return y, yc


def split_heads(t, n_heads):
    return t.reshape(t.shape[:-1] + (n_heads, t.shape[-1] // n_heads))


def merge_heads(t):
    return t.reshape(t.shape[:-2] + (t.shape[-2] * t.shape[-1],))


def rev(t):
    return None if t is None else jnp.flip(t, axis=1)


def l2_normalize(t):
    return t * lax.rsqrt(jnp.sum(t * t, axis=-1, keepdims=True) + EPS)


def centred_conv(t, w):
    pad = CONV_W // 2
    return lax.conv_general_dilated(
        t, w[:, None, :].astype(t.dtype), window_strides=(1,), padding=((pad, pad),),
        dimension_numbers=('NWC', 'WIO', 'NWC'), feature_group_count=t.shape[-1])


def axial_rope_angles(n_tokens, head_dim):
    axis_dim = head_dim // 2
    inv_freq = ROPE_BASE ** (-jnp.arange(0, axis_dim, 2, dtype=F32) / axis_dim)
    pos = jnp.arange(n_tokens)
    row = (pos // GRID_W).astype(F32)
    col = (pos % GRID_W).astype(F32)
    return row[:, None] * inv_freq, col[:, None] * inv_freq


def rope_rotate(t, ang):
    cos = jnp.cos(ang)[:, None, :]
    sin = jnp.sin(ang)[:, None, :]
    t1, t2 = jnp.split(t, 2, axis=-1)
    return jnp.concatenate([t1 * cos - t2 * sin, t2 * cos + t1 * sin], axis=-1)


def apply_rope2d(t, ang):
    ang_row, ang_col = ang
    tf = t.astype(F32)
    half = tf.shape[-1] // 2
    return jnp.concatenate([rope_rotate(tf[..., :half], ang_row), rope_rotate(tf[..., half:], ang_col)], axis=-1)


def gated_head_norm_ref(o, z, w):
    y = o * lax.rsqrt(jnp.mean(o * o, axis=-1, keepdims=True) + EPS) * w.astype(F32)
    return merge_heads(y) * jax.nn.silu(z.astype(F32))


def bidirectional_with_prefix(scan_fn, lat_fwd, lat_bwd, ctx_fwd, ctx_bwd, s0):
    oc_f, sc_f = scan_fn(*ctx_fwd, s0)
    ol_f, _ = scan_fn(*lat_fwd, sc_f)
    oc_b, sc_b = scan_fn(*[rev(t) for t in ctx_bwd], s0)
    ol_b, _ = scan_fn(*[rev(t) for t in lat_bwd], sc_b)
    o_lat = ol_f + rev(ol_b)
    o_ctx = None if oc_f is None else oc_f + rev(oc_b)
    return o_lat, o_ctx


def chunk_gated_delta(q, k, v, g, beta, s0):
    bsz, n_tok, n_heads, _ = k.shape
    n = n_tok // CHUNK
    to_chunks = lambda t: t.reshape(bsz, n, CHUNK, n_heads, t.shape[-1]).transpose(1, 0, 3, 2, 4)
    kc, vc = to_chunks(k), to_chunks(v)
    gc = jnp.cumsum(g.reshape(bsz, n, CHUNK, n_heads).transpose(1, 0, 3, 2), axis=-1)
    bc = beta.reshape(bsz, n, CHUNK, n_heads).transpose(1, 0, 3, 2)
    causal = jnp.tril(jnp.ones((CHUNK, CHUNK), bool))
    strict = jnp.tril(jnp.ones((CHUNK, CHUNK), bool), -1)
    decay = jnp.exp(jnp.where(causal, gc[..., :, None] - gc[..., None, :], -jnp.inf))
    kb = kc * bc[..., None]
    lower = jnp.where(strict, jnp.einsum('nbhik,nbhjk->nbhij', kb, kc) * decay, 0.0)
    unit_lower = lower + jnp.eye(CHUNK, dtype=lower.dtype)
    u = lax.linalg.triangular_solve(unit_lower, vc * bc[..., None], left_side=True, lower=True, unit_diagonal=True)
    w = lax.linalg.triangular_solve(unit_lower, kb * jnp.exp(gc)[..., None], left_side=True, lower=True,
                                    unit_diagonal=True)
    g_last = gc[..., -1]
    k_dec = kc * jnp.exp(g_last[..., None] - gc)[..., None]

    def new_values(s, u_i, w_i):
        return u_i - jnp.einsum('bhck,bhkv->bhcv', w_i, s)

    def update(s, v_new, kd_i, gl_i):
        return s * jnp.exp(gl_i)[..., None, None] + jnp.einsum('bhck,bhcv->bhkv', kd_i, v_new)

    if q is None:
        def state_step(s, xs):
            u_i, w_i, kd_i, gl_i = xs
            return update(s, new_values(s, u_i, w_i), kd_i, gl_i), None
        s_fin, _ = lax.scan(state_step, s0, (u, w, k_dec, g_last))
        return None, s_fin

    qc = to_chunks(q)
    q_dec = qc * jnp.exp(gc)[..., None]
    qk = jnp.where(causal, jnp.einsum('nbhik,nbhjk->nbhij', qc, kc) * decay, 0.0)

    def step(s, xs):
        u_i, w_i, kd_i, gl_i, qd_i, qk_i = xs
        v_new = new_values(s, u_i, w_i)
        o = jnp.einsum('bhck,bhkv->bhcv', qd_i, s) + jnp.einsum('bhij,bhjv->bhiv', qk_i, v_new)
        return update(s, v_new, kd_i, gl_i), o

    s_fin, o = lax.scan(step, s0, (u, w, k_dec, g_last, q_dec, qk))
    return o.transpose(1, 0, 3, 2, 4).reshape(bsz, n_tok, n_heads, -1), s_fin


def chunk_gla(q, k, v, log_a, s0):
    bsz, n_tok, n_heads, _ = k.shape
    n = n_tok // CHUNK
    to_chunks = lambda t: t.reshape(bsz, n, CHUNK, n_heads, t.shape[-1]).transpose(1, 0, 3, 2, 4)
    kc, vc = to_chunks(k), to_chunks(v)
    b = jnp.cumsum(to_chunks(log_a), axis=-2)
    b_last = b[..., -1:, :]
    k_dec = kc * jnp.exp(b_last - b)
    d_last = jnp.exp(b_last[..., 0, :])

    def update(s, k_i, v_i, d_i):
        return d_i[..., None] * s + jnp.einsum('bhck,bhcv->bhkv', k_i, v_i)

    if q is None:
        s_fin, _ = lax.scan(lambda s, xs: (update(s, *xs), None), s0, (k_dec, vc, d_last))
        return None, s_fin

    q_dec = to_chunks(q) * jnp.exp(b)
    k_inv = kc * jnp.exp(-b)
    causal = jnp.tril(jnp.ones((CHUNK, CHUNK), bool))
    scores = jnp.where(causal, jnp.einsum('nbhik,nbhjk->nbhij', q_dec, k_inv), 0.0)
    o_intra = jnp.einsum('nbhij,nbhjv->nbhiv', scores, vc)

    def step(s, xs):
        q_i, k_i, v_i, d_i = xs
        return update(s, k_i, v_i, d_i), jnp.einsum('bhck,bhkv->bhcv', q_i, s)

    s_fin, o_inter = lax.scan(step, s0, (q_dec, k_dec, vc, d_last))
    o = (o_intra + o_inter).transpose(1, 0, 3, 2, 4).reshape(bsz, n_tok, n_heads, -1)
    return o, s_fin


def gdn_prepare(pg, conv_w, a_log, dt_bias):
    bsz, n_tok, _ = pg['gdn_k'].shape
    conv_q, conv_k, conv_v = conv_w[:, :GDN_KW], conv_w[:, GDN_KW:2 * GDN_KW], conv_w[:, 2 * GDN_KW:]

    def conv_act(t, w):
        return split_heads(jax.nn.silu(centred_conv(t.astype(F32), w)), GDN_HEADS)

    k = l2_normalize(conv_act(pg['gdn_k'], conv_k))
    v = conv_act(pg['gdn_v'], conv_v)
    q = l2_normalize(conv_act(pg['gdn_q'], conv_q)) * GDN_DK ** -0.5 if 'gdn_q' in pg else None
    beta = jax.nn.sigmoid(pg['gdn_beta'].astype(F32)).reshape(bsz, n_tok, 2, GDN_HEADS)
    g = -jnp.exp(a_log.astype(F32)) * jax.nn.softplus(
        pg['gdn_a'].astype(F32).reshape(bsz, n_tok, 2, GDN_HEADS) + dt_bias.astype(F32))
    return q, k, v, beta, g


def gdn_mixer(pl_, pc, conv_w, a_log, dt_bias, norm_w, ctx_out):
    ql, kl, vl, bl, gl = gdn_prepare(pl_, conv_w, a_log, dt_bias)
    qc, kc, vc, bc, gc = gdn_prepare(pc, conv_w, a_log, dt_bias)
    s0 = jnp.zeros((kl.shape[0], GDN_HEADS, GDN_DK, GDN_DV), F32)
    o_lat, o_ctx = bidirectional_with_prefix(
        chunk_gated_delta,
        (ql, kl, vl, gl[:, :, 0], bl[:, :, 0]), (ql, kl, vl, gl[:, :, 1], bl[:, :, 1]),
        (qc, kc, vc, gc[:, :, 0], bc[:, :, 0]), (qc, kc, vc, gc[:, :, 1], bc[:, :, 1]), s0)
    y = gated_head_norm_ref(o_lat, pl_['gdn_z'], norm_w)
    yc = gated_head_norm_ref(o_ctx, pc['gdn_z'], norm_w) if ctx_out else None
    return y, yc


def gla_prepare(pg, w2, b2, ang):
    bsz, n_tok, _ = pg['gla_k'].shape

    def qk_heads(t):
        t = split_heads(t.astype(F32), GLA_HEADS)
        if ang is not None:
            t = apply_rope2d(t, ang)
        return t

    k = qk_heads(pg['gla_k'])
    q = qk_heads(pg['gla_q']) * GLA_DK ** -0.5 if 'gla_q' in pg else None
    v = split_heads(pg['gla_v'], GLA_HEADS).astype(F32)
    lr = pg['gla_lr'].astype(F32).reshape(bsz, n_tok, 2, GLA_LOWRANK)
    log_a = jax.nn.log_sigmoid(jnp.einsum('btzr,zrk->btzk', lr, w2.astype(F32)) + b2.astype(F32)) / GLA_GATE_NORM
    return q, k, v, log_a.reshape(bsz, n_tok, 2, GLA_HEADS, GLA_DK)


def gla_mixer(pl_, pc, w2, b2, norm_w, ang, ctx_out):
    ql, kl, vl, al = gla_prepare(pl_, w2, b2, ang)
    qc, kc, vc, ac = gla_prepare(pc, w2, b2, None)
    s0 = jnp.zeros((kl.shape[0], GLA_HEADS, GLA_DK, GLA_DV), F32)
    o_lat, o_ctx = bidirectional_with_prefix(
        chunk_gla,
        (ql, kl, vl, al[:, :, 0]), (ql, kl, vl, al[:, :, 1]),
        (qc, kc, vc, ac[:, :, 0]), (qc, kc, vc, ac[:, :, 1]), s0)
    y = gated_head_norm_ref(o_lat, pl_['gla_z'], norm_w)
    yc = gated_head_norm_ref(o_ctx, pc['gla_z'], norm_w) if ctx_out else None
    return y, yc


def split_groups(proj, narrow, drop=()):
    out = {}
    for name in WIDE_GROUPS:
        if name not in drop:
            off, width = WIDE_OFFSET[name]
            out[name] = proj[..., off:off + width]
    for name in NARROW_GROUPS:
        off, width = NARROW_OFFSET[name]
        out[name] = narrow[..., off:off + width]
    return out


def expert_choice_ffn(h, logits, w_gate, w_up, w_down):
    bsz, n_tok, d = h.shape
    cap = max(1, CAPACITY_FACTOR * n_tok // N_EXPERTS)
    affinity = jax.nn.softmax(logits, axis=-1)
    gate, idx = lax.top_k(jnp.swapaxes(affinity, 1, 2), cap)
    xs = jax.vmap(lambda hb, ib: hb[ib])(h, idx)
    ys = expert_ffn(xs, gate[..., None], w_gate, w_up, w_down, tc=512)
    return jax.vmap(lambda yb, ib: jnp.zeros((n_tok, d), F32).at[ib.reshape(-1)].add(yb.reshape(-1, d)))(ys, idx)


def pack_in_weight(w_in):
    wide = jnp.concatenate([w_in[:, IN_OFFSET[n][0]:IN_OFFSET[n][0] + IN_OFFSET[n][1]] for n in WIDE_GROUPS], axis=1)
    narrow = jnp.concatenate([w_in[:, IN_OFFSET[n][0]:IN_OFFSET[n][0] + IN_OFFSET[n][1]] for n in NARROW_GROUPS],
                             axis=1)
    narrow = jnp.pad(narrow, ((0, 0), (0, NARROW_WIDTH - narrow.shape[1])))
    return wide.astype(BF16), narrow.astype(BF16)


def kernel(x, c, ctx, c_ctx, norm1_w, norm2_w, final_norm_w, w_ada, b_ada, w_in, na_rpb, gdn_conv,
           gdn_a_log, gdn_dt_bias, gdn_norm_w, gla_w2, gla_b2, gla_norm_w, w_branch, w_out,
           w_router, w_gate, w_up, w_down):
    bsz, n_lat, d = x.shape
    rows = n_lat // GRID_W
    ang = axial_rope_angles(n_lat, GLA_DK)
    cond_lat = jax.nn.silu(c)
    cond_ctx = jnp.broadcast_to(jax.nn.silu(c_ctx)[None], (bsz, d))
    for layer in range(DEPTH):
        ctx_out = layer < DEPTH - 1
        mod_lat = jnp.split((cond_lat @ w_ada[layer] + b_ada[layer])[:, None, :], 6, axis=-1)
        mod_ctx = jnp.split((cond_ctx @ w_ada[layer] + b_ada[layer])[:, None, :], 6, axis=-1)
        w_wide, w_narrow = pack_in_weight(w_in[layer])
        wb = w_branch[layer].astype(BF16)
        wo = w_out[layer].astype(BF16)
        wg, wu, wd = w_gate[layer].astype(BF16), w_up[layer].astype(BF16), w_down[layer].astype(BF16)
        wr = jnp.pad(w_router[layer], ((0, 0), (0, LANES - N_EXPERTS)))

        proj_l, narrow_l = norm_mod_project(x, norm1_w[layer], mod_lat[0], mod_lat[1], w_wide, w_narrow,
                                            tm=1024, tn=1024)
        proj_c, narrow_c = norm_mod_project(ctx, norm1_w[layer], mod_ctx[0], mod_ctx[1], w_wide, w_narrow,
                                            tm=256, tn=1024)
        ya = neighbourhood_attention(proj_l, proj_c, na_rpb[layer], rows)
        yca = context_attention(proj_c) if ctx_out else None
        pl_ = split_groups(proj_l, narrow_l)
        pc = split_groups(proj_c, narrow_c)
        if not ctx_out:
            pc.pop('gdn_q')
            pc.pop('gla_q')
        yb, ycb = gdn_mixer(pl_, pc, gdn_conv[layer], gdn_a_log[layer], gdn_dt_bias[layer], gdn_norm_w[layer],
                            ctx_out)
        yb = yb.astype(BF16)
        yg, ycg = gla_mixer(pl_, pc, gla_w2[layer], gla_b2[layer], gla_norm_w[layer], ang, ctx_out)
        yg = yg.astype(BF16)

        y = merge_branches(proj_l, ya, yb, yg, wb, tm=1024, tn=1024)
        x = out_proj_residual(y, wo, x, mod_lat[2], tm=1024, tn=1024)
        h2, logits = norm_mod_router(x, norm2_w[layer], mod_lat[3], mod_lat[4], wr, tm=512)
        x = x + mod_lat[5] * expert_choice_ffn(h2, logits[..., :N_EXPERTS], wg, wu, wd)
        if ctx_out:
            yc = merge_branches(proj_c, yca.astype(BF16), ycb.astype(BF16), ycg.astype(BF16), wb, tm=256, tn=1024)
            ctx = out_proj_residual(yc, wo, ctx, mod_ctx[2], tm=256, tn=1024)
            hc2, logits_c = norm_mod_router(ctx, norm2_w[layer], mod_ctx[3], mod_ctx[4], wr, tm=256)
            ctx = ctx + mod_ctx[5] * expert_choice_ffn(hc2, logits_c[..., :N_EXPERTS], wg, wu, wd)
    return final_norm(x, final_norm_w, tm=512)
```

```python
import functools

import jax
import jax.numpy as jnp
import numpy as np
from jax import lax
from jax.experimental import pallas as pl
from jax.experimental.pallas import tpu as pltpu

D_MODEL = 2048
DEPTH = 4
GRID_W = 64

NA_HEADS = 8
NA_DIM = 128
WIN_ROWS = 8
WIN_COLS = 16
Q_COL_BLOCK = 16
KEY_COL_BAND = 32

GDN_HEADS = 8
GDN_DK = 128
GDN_DV = 128
CONV_W = 5

GLA_HEADS = 4
GLA_DK = 128
GLA_DV = 256
GLA_LOWRANK = 16
GLA_GATE_NORM = 16.0

CHUNK = 64
ROPE_BASE = 10000.0

N_BRANCH = 3
BRANCH_W = 1024

N_EXPERTS = 16
EXPERT_FF = 1024
CAPACITY_FACTOR = 2

EPS = 1e-6
F32 = jnp.float32
BF16 = jnp.bfloat16

NA_W = NA_HEADS * NA_DIM
GDN_KW = GDN_HEADS * GDN_DK
GDN_VW = GDN_HEADS * GDN_DV
GLA_KW = GLA_HEADS * GLA_DK
GLA_VW = GLA_HEADS * GLA_DV

IN_GROUPS = (
    ('na_q', NA_W), ('na_k', NA_W), ('na_v', NA_W),
    ('gdn_q', GDN_KW), ('gdn_k', GDN_KW), ('gdn_v', GDN_VW), ('gdn_z', GDN_VW),
    ('gdn_beta', 2 * GDN_HEADS), ('gdn_a', 2 * GDN_HEADS),
    ('gla_q', GLA_KW), ('gla_k', GLA_KW), ('gla_v', GLA_VW), ('gla_z', GLA_VW),
    ('gla_lr', 2 * GLA_LOWRANK),
    ('gate_na', D_MODEL), ('gate_gdn', D_MODEL), ('gate_gla', D_MODEL),
)
IN_OFFSET = {}
_off = 0
for _name, _width in IN_GROUPS:
    IN_OFFSET[_name] = (_off, _width)
    _off += _width

WIDE_GROUPS = ('na_q', 'na_k', 'na_v', 'gdn_q', 'gdn_k', 'gdn_v', 'gdn_z',
               'gla_q', 'gla_k', 'gla_v', 'gla_z', 'gate_na', 'gate_gdn', 'gate_gla')
NARROW_GROUPS = ('gdn_beta', 'gdn_a', 'gla_lr')
WIDE_OFFSET = {}
_off = 0
for _name in WIDE_GROUPS:
    WIDE_OFFSET[_name] = (_off, IN_OFFSET[_name][1])
    _off += IN_OFFSET[_name][1]
WIDE_WIDTH = _off
NARROW_OFFSET = {}
_off = 0
for _name in NARROW_GROUPS:
    NARROW_OFFSET[_name] = (_off, IN_OFFSET[_name][1])
    _off += IN_OFFSET[_name][1]
LANES = 128
NARROW_WIDTH = LANES

VMEM_LIMIT = 56 * 1024 * 1024


def _cparams(*sem):
    return pltpu.CompilerParams(dimension_semantics=sem, vmem_limit_bytes=VMEM_LIMIT)


def _norm_mod_rows(x, nw, shift, scale):
    y = x * lax.rsqrt(jnp.mean(x * x, axis=-1, keepdims=True) + EPS) * nw
    return y * (1.0 + scale) + shift


def _proj_kernel(x_ref, nw_ref, sh_ref, sc_ref, w_ref, wn_ref, o_ref, on_ref, h_scr, *, row_chunk):
    n = pl.program_id(2)
    tm = x_ref.shape[1]

    @pl.when(n == 0)
    def _():
        def body(i, carry):
            rows = pl.ds(pl.multiple_of(i * row_chunk, row_chunk), row_chunk)
            h = _norm_mod_rows(x_ref[0, rows, :], nw_ref[...], sh_ref[0], sc_ref[0])
            h_scr[rows, :] = h.astype(BF16)
            return carry
        lax.fori_loop(0, tm // row_chunk, body, 0)
        on_ref[0] = jnp.dot(h_scr[...], wn_ref[...], preferred_element_type=F32)

    o_ref[0] = jnp.dot(h_scr[...], w_ref[...], preferred_element_type=F32).astype(o_ref.dtype)


def norm_mod_project(x, nw, shift, scale, w_wide, w_narrow, *, tm, tn):
    bsz, n_tok, d = x.shape
    n_wide = w_wide.shape[1]
    n_narrow = w_narrow.shape[1]
    tm = min(tm, n_tok)
    row_chunk = min(128, tm)
    grid = (bsz, n_tok // tm, n_wide // tn)
    return pl.pallas_call(
        functools.partial(_proj_kernel, row_chunk=row_chunk),
        grid=grid,
        in_specs=[
            pl.BlockSpec((1, tm, d), lambda b, m, n: (b, m, 0)),
            pl.BlockSpec((1, d), lambda b, m, n: (0, 0)),
            pl.BlockSpec((1, 1, d), lambda b, m, n: (b, 0, 0)),
            pl.BlockSpec((1, 1, d), lambda b, m, n: (b, 0, 0)),
            pl.BlockSpec((d, tn), lambda b, m, n: (0, n)),
            pl.BlockSpec((d, n_narrow), lambda b, m, n: (0, 0)),
        ],
        out_specs=[
            pl.BlockSpec((1, tm, tn), lambda b, m, n: (b, m, n)),
            pl.BlockSpec((1, tm, n_narrow), lambda b, m, n: (b, m, 0)),
        ],
        out_shape=[
            jax.ShapeDtypeStruct((bsz, n_tok, n_wide), BF16),
            jax.ShapeDtypeStruct((bsz, n_tok, n_narrow), F32),
        ],
        scratch_shapes=[pltpu.VMEM((tm, d), BF16)],
        compiler_params=_cparams("parallel", "parallel", "arbitrary"),
        name="norm_mod_project",
    )(x, nw.reshape(1, d), shift, scale, w_wide, w_narrow)


def _merge_kernel(ga_ref, gb_ref, gg_ref, ya_ref, yb_ref, yg_ref, wb_ref, o_ref):
    acc = jax.nn.sigmoid(ga_ref[0].astype(F32)) * jnp.dot(ya_ref[0], wb_ref[0], preferred_element_type=F32)
    acc += jax.nn.sigmoid(gb_ref[0].astype(F32)) * jnp.dot(yb_ref[0], wb_ref[1], preferred_element_type=F32)
    acc += jax.nn.sigmoid(gg_ref[0].astype(F32)) * jnp.dot(yg_ref[0], wb_ref[2], preferred_element_type=F32)
    o_ref[0] = acc.astype(o_ref.dtype)


def merge_branches(proj, ya, yb, yg, w_branch, *, tm, tn):
    bsz, n_tok, _ = proj.shape
    d = w_branch.shape[2]
    tm = min(tm, n_tok)
    gate_blk = [WIDE_OFFSET[name][0] // tn for name in ('gate_na', 'gate_gdn', 'gate_gla')]

    def gate_spec(blk):
        return pl.BlockSpec((1, tm, tn), lambda b, m, n: (b, m, blk + n))

    br_spec = pl.BlockSpec((1, tm, BRANCH_W), lambda b, m, n: (b, m, 0))
    return pl.pallas_call(
        _merge_kernel,
        grid=(bsz, n_tok // tm, d // tn),
        in_specs=[gate_spec(gate_blk[0]), gate_spec(gate_blk[1]), gate_spec(gate_blk[2]),
                  br_spec, br_spec, br_spec,
                  pl.BlockSpec((N_BRANCH, BRANCH_W, tn), lambda b, m, n: (0, 0, n))],
        out_specs=pl.BlockSpec((1, tm, tn), lambda b, m, n: (b, m, n)),
        out_shape=jax.ShapeDtypeStruct((bsz, n_tok, d), BF16),
        compiler_params=_cparams("parallel", "parallel", "arbitrary"),
        name="merge_branches",
    )(proj, proj, proj, ya, yb, yg, w_branch)


def _out_residual_kernel(y_ref, w_ref, x_ref, g_ref, o_ref):
    o_ref[0] = x_ref[0] + g_ref[0] * jnp.dot(y_ref[0], w_ref[...], preferred_element_type=F32)


def out_proj_residual(y, w_out, x, gate, *, tm, tn):
    bsz, n_tok, d = x.shape
    tm = min(tm, n_tok)
    return pl.pallas_call(
        _out_residual_kernel,
        grid=(bsz, n_tok // tm, d // tn),
        in_specs=[
            pl.BlockSpec((1, tm, d), lambda b, m, n: (b, m, 0)),
            pl.BlockSpec((d, tn), lambda b, m, n: (0, n)),
            pl.BlockSpec((1, tm, tn), lambda b, m, n: (b, m, n)),
            pl.BlockSpec((1, 1, tn), lambda b, m, n: (b, 0, n)),
        ],
        out_specs=pl.BlockSpec((1, tm, tn), lambda b, m, n: (b, m, n)),
        out_shape=jax.ShapeDtypeStruct((bsz, n_tok, d), F32),
        compiler_params=_cparams("parallel", "parallel", "arbitrary"),
        name="out_proj_residual",
    )(y, w_out, x, gate)


def _norm_router_kernel(x_ref, nw_ref, sh_ref, sc_ref, wr_ref, h_ref, l_ref):
    h = _norm_mod_rows(x_ref[0], nw_ref[...], sh_ref[0], sc_ref[0])
    h_ref[0] = h.astype(h_ref.dtype)
    l_ref[0] = jnp.dot(h, wr_ref[...], preferred_element_type=F32, precision=lax.Precision.HIGHEST)


def norm_mod_router(x, nw, shift, scale, w_router_pad, *, tm):
    bsz, n_tok, d = x.shape
    n_pad = w_router_pad.shape[1]
    tm = min(tm, n_tok)
    return pl.pallas_call(
        _norm_router_kernel,
        grid=(bsz, n_tok // tm),
        in_specs=[
            pl.BlockSpec((1, tm, d), lambda b, m: (b, m, 0)),
            pl.BlockSpec((1, d), lambda b, m: (0, 0)),
            pl.BlockSpec((1, 1, d), lambda b, m: (b, 0, 0)),
            pl.BlockSpec((1, 1, d), lambda b, m: (b, 0, 0)),
            pl.BlockSpec((d, n_pad), lambda b, m: (0, 0)),
        ],
        out_specs=[
            pl.BlockSpec((1, tm, d), lambda b, m: (b, m, 0)),
            pl.BlockSpec((1, tm, n_pad), lambda b, m: (b, m, 0)),
        ],
        out_shape=[
            jax.ShapeDtypeStruct((bsz, n_tok, d), BF16),
            jax.ShapeDtypeStruct((bsz, n_tok, n_pad), F32),
        ],
        compiler_params=_cparams("parallel", "parallel"),
        name="norm_mod_router",
    )(x, nw.reshape(1, d), shift, scale, w_router_pad)


def _expert_ffn_kernel(xs_ref, g_ref, wg_ref, wu_ref, wd_ref, o_ref):
    xs = xs_ref[0, 0]
    hid = jax.nn.silu(jnp.dot(xs, wg_ref[0], preferred_element_type=F32)) * jnp.dot(
        xs, wu_ref[0], preferred_element_type=F32)
    ys = jnp.dot(hid.astype(BF16), wd_ref[0], preferred_element_type=F32)
    o_ref[0, 0] = ys * g_ref[0, 0]


def expert_ffn(xs, gate, w_gate, w_up, w_down, *, tc):
    bsz, n_exp, cap, d = xs.shape
    ff = w_gate.shape[2]
    tc = min(tc, cap)
    return pl.pallas_call(
        _expert_ffn_kernel,
        grid=(n_exp, bsz, cap // tc),
        in_specs=[
            pl.BlockSpec((1, 1, tc, d), lambda e, b, c: (b, e, c, 0)),
            pl.BlockSpec((1, 1, tc, 1), lambda e, b, c: (b, e, c, 0)),
            pl.BlockSpec((1, d, ff), lambda e, b, c: (e, 0, 0)),
            pl.BlockSpec((1, d, ff), lambda e, b, c: (e, 0, 0)),
            pl.BlockSpec((1, ff, d), lambda e, b, c: (e, 0, 0)),
        ],
        out_specs=pl.BlockSpec((1, 1, tc, d), lambda e, b, c: (b, e, c, 0)),
        out_shape=jax.ShapeDtypeStruct((bsz, n_exp, cap, d), F32),
        compiler_params=_cparams("parallel", "parallel", "arbitrary"),
        name="expert_ffn",
    )(xs, gate, w_gate, w_up, w_down)


def _final_norm_kernel(x_ref, w_ref, o_ref):
    x = x_ref[0]
    o_ref[0] = x * lax.rsqrt(jnp.mean(x * x, axis=-1, keepdims=True) + EPS) * w_ref[...]


def final_norm(x, w, *, tm):
    bsz, n_tok, d = x.shape
    return pl.pallas_call(
        _final_norm_kernel,
        grid=(bsz, n_tok // tm),
        in_specs=[pl.BlockSpec((1, tm, d), lambda b, m: (b, m, 0)),
                  pl.BlockSpec((1, d), lambda b, m: (0, 0))],
        out_specs=pl.BlockSpec((1, tm, d), lambda b, m: (b, m, 0)),
        out_shape=jax.ShapeDtypeStruct((bsz, n_tok, d), F32),
        compiler_params=_cparams("parallel", "parallel"),
        name="final_norm",
    )(x, w.reshape(1, d))


NA_Q_ROWS = 8
NA_KEY_SLOTS = 4
NA_KV_ROWS = 4
NA_Q_TOK = NA_Q_ROWS * GRID_W
NA_KV_TOK = NA_KV_ROWS * GRID_W
NA_Q_CHUNK = 128


def na_bias_tables(rpb, rows):
    n_steps = rows // NA_Q_ROWS
    n_key_rows = NA_KEY_SLOTS * NA_KV_ROWS
    c = np.arange(GRID_W)
    ws = np.clip(c - WIN_COLS // 2, 0, GRID_W - WIN_COLS)
    col_ok = (c[None, :] >= ws[:, None]) & (c[None, :] < ws[:, None] + WIN_COLS)
    dc = np.clip(c[None, :] - c[:, None] + WIN_COLS - 1, 0, 2 * WIN_COLS - 2)
    pick_c = (np.arange(2 * WIN_COLS - 1) == dc[..., None]).astype(np.float32)
    tables = []
    for g in (0, 1, n_steps - 1):
        r = NA_Q_ROWS * g + np.arange(NA_Q_ROWS)
        kr = NA_Q_ROWS * g - WIN_ROWS // 2 + np.arange(n_key_rows)
        rs = np.clip(r - WIN_ROWS // 2, 0, rows - WIN_ROWS)
        row_ok = (kr[None, :] >= rs[:, None]) & (kr[None, :] < rs[:, None] + WIN_ROWS)
        dr = np.clip(kr[None, :] - r[:, None] + WIN_ROWS - 1, 0, 2 * WIN_ROWS - 2)
        pick_r = (np.arange(2 * WIN_ROWS - 1) == dr[..., None]).astype(np.float32)
        bias = jnp.einsum('abi,hij,cdj->hacbd', pick_r, rpb.astype(F32), pick_c, precision=HIGHEST)
        valid = row_ok[:, None, :, None] & col_ok[None, :, None, :]
        table = jnp.where(valid[None], bias, -jnp.inf)
        tables.append(table.reshape(rpb.shape[0], NA_Q_TOK, n_key_rows * GRID_W))
    return jnp.stack(tables)


def _na_kernel(q_ref, k0, k1, k2, k3, v0, v1, v2, v3, kc_ref, vc_ref, tbl_ref, o_ref):
    scale = NA_DIM ** -0.5
    k_refs = (k0, k1, k2, k3, kc_ref)
    v_refs = (v0, v1, v2, v3, vc_ref)
    contract_last = (((1,), (1,)), ((), ()))

    def body(i, carry):
        rows = pl.ds(pl.multiple_of(i * NA_Q_CHUNK, NA_Q_CHUNK), NA_Q_CHUNK)
        q = q_ref[0, rows, :]
        scores = []
        for s, k_ref in enumerate(k_refs):
            sc = lax.dot_general(q, k_ref[0], contract_last, preferred_element_type=F32) * scale
            if s < NA_KEY_SLOTS:
                sc = sc + tbl_ref[0, 0, rows, s * NA_KV_TOK:(s + 1) * NA_KV_TOK]
            scores.append(sc)
        m = scores[0].max(axis=-1, keepdims=True)
        for sc in scores[1:]:
            m = jnp.maximum(m, sc.max(axis=-1, keepdims=True))
        probs = [jnp.exp(sc - m) for sc in scores]
        denom = probs[0].sum(axis=-1, keepdims=True)
        for p in probs[1:]:
            denom = denom + p.sum(axis=-1, keepdims=True)
        inv = 1.0 / denom
        acc = jnp.zeros((NA_Q_CHUNK, NA_DIM), F32)
        for p, v_ref in zip(probs, v_refs):
            acc = acc + jnp.dot((p * inv).astype(BF16), v_ref[0], preferred_element_type=F32)
        o_ref[0, rows, :] = acc.astype(o_ref.dtype)
        return carry

    lax.fori_loop(0, NA_Q_TOK // NA_Q_CHUNK, body, 0, unroll=2)


def neighbourhood_attention(proj_l, proj_c, rpb, rows):
    bsz, n_tok, _ = proj_l.shape
    n_ctx = proj_c.shape[1]
    n_steps = rows // NA_Q_ROWS
    n_kv_blocks = n_tok // NA_KV_TOK
    assert n_steps >= 2 and rows % NA_Q_ROWS == 0
    tables = na_bias_tables(rpb, rows)
    q_blk, k_blk, v_blk = (WIDE_OFFSET[n][0] // NA_DIM for n in ('na_q', 'na_k', 'na_v'))

    def kv_spec(col_blk, slot):
        def idx(h, b, g):
            blk = jnp.clip(2 * g - 1 + slot, 0, n_kv_blocks - 1)
            return (b, blk, col_blk + h)
        return pl.BlockSpec((1, NA_KV_TOK, NA_DIM), idx)

    def tbl_idx(h, b, g):
        kind = jnp.where(g == 0, 0, jnp.where(g == n_steps - 1, 2, 1))
        return (kind, h, 0, 0)

    return pl.pallas_call(
        _na_kernel,
        grid=(NA_HEADS, bsz, n_steps),
        in_specs=[pl.BlockSpec((1, NA_Q_TOK, NA_DIM), lambda h, b, g: (b, g, q_blk + h))]
        + [kv_spec(k_blk, s) for s in range(NA_KEY_SLOTS)]
        + [kv_spec(v_blk, s) for s in range(NA_KEY_SLOTS)]
        + [pl.BlockSpec((1, n_ctx, NA_DIM), lambda h, b, g: (b, 0, k_blk + h)),
           pl.BlockSpec((1, n_ctx, NA_DIM), lambda h, b, g: (b, 0, v_blk + h)),
           pl.BlockSpec((1, 1, NA_Q_TOK, NA_KEY_SLOTS * NA_KV_TOK), tbl_idx)],
        out_specs=pl.BlockSpec((1, NA_Q_TOK, NA_DIM), lambda h, b, g: (b, g, h)),
        out_shape=jax.ShapeDtypeStruct((bsz, n_tok, NA_W), BF16),
        compiler_params=_cparams("parallel", "parallel", "arbitrary"),
        name="neighbourhood_attention",
    )(proj_l, *([proj_l] * (2 * NA_KEY_SLOTS)), proj_c, proj_c, tables)


def _ctx_attn_kernel(q_ref, k_ref, v_ref, o_ref):
    s = lax.dot_general(q_ref[0], k_ref[0], (((1,), (1,)), ((), ())), preferred_element_type=F32) * NA_DIM ** -0.5
    p = jnp.exp(s - s.max(axis=-1, keepdims=True))
    p = p / p.sum(axis=-1, keepdims=True)
    o_ref[0] = jnp.dot(p.astype(BF16), v_ref[0], preferred_element_type=F32).astype(o_ref.dtype)


def context_attention(proj_c):
    bsz, n_ctx, _ = proj_c.shape
    q_blk, k_blk, v_blk = (WIDE_OFFSET[n][0] // NA_DIM for n in ('na_q', 'na_k', 'na_v'))

    def spec(col_blk):
        return pl.BlockSpec((1, n_ctx, NA_DIM), lambda b, h: (b, 0, col_blk + h))

    return pl.pallas_call(
        _ctx_attn_kernel,
        grid=(bsz, NA_HEADS),
        in_specs=[spec(q_blk), spec(k_blk), spec(v_blk)],
        out_specs=pl.BlockSpec((1, n_ctx, NA_DIM), lambda b, h: (b, 0, h)),
        out_shape=jax.ShapeDtypeStruct((bsz, n_ctx, NA_W), BF16),
        compiler_params=_cparams("parallel", "parallel"),
        name="context_attention",
    )(proj_c, proj_c, proj_c)


HIGHEST = lax.Precision.HIGHEST
NT_DIMS = (((1,), (1,)), ((), ()))
TN_DIMS = (((0,), (0,)), ((), ()))
SCAN_TB = 512


def _scan_masks(d):
    r = lax.broadcasted_iota(jnp.int32, (CHUNK, CHUNK), 0)
    c = lax.broadcasted_iota(jnp.int32, (CHUNK, CHUNK), 1)
    diff = (r - c) * jnp.where(d == 0, 1, -1)
    return diff >= 0, diff > 0


def _softplus(x):
    return jnp.maximum(x, 0.0) + jnp.log(1.0 + jnp.exp(-jnp.abs(x)))


def _gdn_conv_kernel(x_ref, prev_ref, next_ref, w_ref, o_ref, ext_scr):
    m = pl.program_id(1)
    s = pl.program_id(2)
    n_m = pl.num_programs(1)
    tm = x_ref.shape[1]
    halo = 8
    pad = CONV_W // 2
    prev = prev_ref[0].astype(F32)[halo:, :]
    nxt = next_ref[0].astype(F32)[:halo, :]
    ext_scr[0:halo, :] = jnp.where(m > 0, prev, 0.0)
    ext_scr[halo:halo + tm, :] = x_ref[0].astype(F32)
    ext_scr[halo + tm:2 * halo + tm, :] = jnp.where(m < n_m - 1, nxt, 0.0)
    q_scale = jnp.where(s == 0, GDN_DK ** -0.5, 1.0)
    for h in range(GDN_HEADS):
        hs = slice(h * GDN_DK, (h + 1) * GDN_DK)
        acc = jnp.zeros((tm, GDN_DK), F32)
        for j in range(CONV_W):
            acc = acc + ext_scr[pl.ds(halo - pad + j, tm), hs] * w_ref[j:j + 1, hs]
        y = acc * jax.nn.sigmoid(acc)
        normed = y * lax.rsqrt(jnp.sum(y * y, axis=-1, keepdims=True) + EPS) * q_scale
        o_ref[0, :, hs] = jnp.where(s < 2, normed, y).astype(o_ref.dtype)


def gdn_conv_act(proj, conv_w8, *, tm):
    bsz, n_tok, _ = proj.shape
    tm = min(tm, n_tok)
    halo_rows = 16
    col0 = WIDE_OFFSET['gdn_q'][0] // GDN_KW
    n_halo = n_tok // halo_rows
    per = tm // halo_rows
    return pl.pallas_call(
        _gdn_conv_kernel,
        grid=(bsz, n_tok // tm, 3),
        in_specs=[
            pl.BlockSpec((1, tm, GDN_KW), lambda b, m, s: (b, m, col0 + s)),
            pl.BlockSpec((1, halo_rows, GDN_KW), lambda b, m, s: (b, jnp.maximum(m * per - 1, 0), col0 + s)),
            pl.BlockSpec((1, halo_rows, GDN_KW), lambda b, m, s: (b, jnp.minimum((m + 1) * per, n_halo - 1), col0 + s)),
            pl.BlockSpec((8, GDN_KW), lambda b, m, s: (0, s)),
        ],
        out_specs=pl.BlockSpec((1, tm, GDN_KW), lambda b, m, s: (b, m, s)),
        out_shape=jax.ShapeDtypeStruct((bsz, n_tok, 3 * GDN_KW), BF16),
        scratch_shapes=[pltpu.VMEM((tm + 16, GDN_KW), F32)],
        compiler_params=_cparams("parallel", "parallel", "arbitrary"),
        name="gdn_conv",
    )(proj, proj, proj, conv_w8)


def _gdn_feat_kernel(n_ref, alog_ref, dt_ref, sel_ref, f_ref):
    x = n_ref[0]
    tm = x.shape[0]
    beta = jax.nn.sigmoid(x)
    g = -jnp.exp(alog_ref[...]) * _softplus(x + dt_ref[...])
    r = lax.broadcasted_iota(jnp.int32, (tm, tm), 0)
    c = lax.broadcasted_iota(jnp.int32, (tm, tm), 1)
    same = (r // CHUNK) == (c // CHUNK)
    dot = functools.partial(jnp.dot, precision=HIGHEST, preferred_element_type=F32)
    prefix = dot((same & (r >= c)).astype(F32), g)
    suffix = dot((same & (r <= c)).astype(F32), g)
    total = dot(same.astype(F32), g)
    for d, cum in enumerate((prefix, suffix)):
        f_ref[d, 0] = dot(beta, sel_ref[d, 0]) + dot(cum, sel_ref[d, 1]) + dot(total, sel_ref[d, 2])


def gdn_feature_select():
    sel = jnp.zeros((2, 3, LANES, LANES), F32)
    beta0, a0 = NARROW_OFFSET['gdn_beta'][0], NARROW_OFFSET['gdn_a'][0]
    h = jnp.arange(GDN_HEADS)
    for d in range(2):
        sel = sel.at[d, 0, beta0 + GDN_HEADS * d + h, h].set(1.0)
        sel = sel.at[d, 1, a0 + GDN_HEADS * d + h, GDN_HEADS + h].set(1.0)
        sel = sel.at[d, 2, a0 + GDN_HEADS * d + h, 2 * GDN_HEADS + h].set(1.0)
    return sel


def gdn_features(narrow, a_log, dt_bias, *, tm=256):
    bsz, n_tok, _ = narrow.shape
    a0 = NARROW_OFFSET['gdn_a'][0]
    alog_row = jnp.zeros((1, LANES), F32).at[0, a0:a0 + 2 * GDN_HEADS].set(a_log.reshape(-1))
    dt_row = jnp.zeros((1, LANES), F32).at[0, a0:a0 + 2 * GDN_HEADS].set(dt_bias.reshape(-1))
    return pl.pallas_call(
        _gdn_feat_kernel,
        grid=(bsz, n_tok // tm),
        in_specs=[
            pl.BlockSpec((1, tm, LANES), lambda b, m: (b, m, 0)),
            pl.BlockSpec((1, LANES), lambda b, m: (0, 0)),
            pl.BlockSpec((1, LANES), lambda b, m: (0, 0)),
            pl.BlockSpec((2, 3, LANES, LANES), lambda b, m: (0, 0, 0, 0)),
        ],
        out_specs=pl.BlockSpec((2, 1, tm, LANES), lambda b, m: (0, b, m, 0)),
        out_shape=jax.ShapeDtypeStruct((2, bsz, n_tok, LANES), F32),
        compiler_params=_cparams("parallel", "parallel"),
        name="gdn_features",
    )(narrow, alog_row, dt_row, gdn_feature_select())


def _gdn_scan_kernel(q_ref, k_ref, v_ref, f_ref, gct_ref, s0_ref, o_ref, sfin_ref, s_scr, *, n_chunks):
    d = pl.program_id(1)
    i = pl.program_id(2)

    @pl.when(i == 0)
    def _():
        s_scr[...] = s0_ref[0, 0]

    incl, strict = _scan_masks(d)
    eye = (lax.broadcasted_iota(jnp.int32, (CHUNK, CHUNK), 0)
           == lax.broadcasted_iota(jnp.int32, (CHUNK, CHUNK), 1)).astype(F32)
    dot = functools.partial(jnp.dot, preferred_element_type=F32)

    def body(ci, carry):
        cc = jnp.where(d == 0, ci, n_chunks - 1 - ci)
        rows = pl.ds(pl.multiple_of(cc * CHUNK, CHUNK), CHUNK)
        feat = f_ref[0, 0, rows, :]
        gct = gct_ref[0, 0, cc]
        for h in range(GDN_HEADS):
            hs = slice(h * GDN_DK, (h + 1) * GDN_DK)
            beta = feat[:, h:h + 1]
            gc = feat[:, GDN_HEADS + h:GDN_HEADS + h + 1]
            gtot = feat[:, 2 * GDN_HEADS + h:2 * GDN_HEADS + h + 1]
            decay = jnp.where(incl, jnp.exp(jnp.where(incl, gc - gct[h:h + 1, :], 0.0)), 0.0)
            qh, kh, vh = q_ref[0, rows, hs], k_ref[0, rows, hs], v_ref[0, rows, hs]
            kf = kh.astype(F32)
            kk = lax.dot_general(kh, kh, NT_DIMS, preferred_element_type=F32)
            x = -jnp.where(strict, beta * kk * decay, 0.0)
            inv = eye + x
            power = x
            for _ in range(5):
                pb = power.astype(BF16)
                power = dot(pb, pb)
                inv = inv + dot(inv.astype(BF16), power.astype(BF16))
            inv = inv.astype(BF16)
            egc = jnp.exp(gc)
            u = dot(inv, (vh.astype(F32) * beta).astype(BF16))
            w = dot(inv, (kf * (beta * egc)).astype(BF16))
            k_dec = (kf * jnp.exp(gtot - gc)).astype(BF16)
            q_dec = (qh.astype(F32) * egc).astype(BF16)
            qk = jnp.where(incl, lax.dot_general(qh, kh, NT_DIMS, preferred_element_type=F32) * decay, 0.0)
            s = s_scr[h]
            sb = s.astype(BF16)
            v_new = u - dot(w.astype(BF16), sb)
            vb = v_new.astype(BF16)
            o_ref[0, 0, rows, hs] = dot(q_dec, sb) + dot(qk.astype(BF16), vb)
            s_scr[h] = s * jnp.exp(gtot[0:1, 0:1]) + lax.dot_general(k_dec, vb, TN_DIMS, preferred_element_type=F32)
        return carry

    lax.fori_loop(0, n_chunks, body, 0)

    @pl.when(i == pl.num_programs(2) - 1)
    def _():
        sfin_ref[0, 0] = s_scr[...]


def gdn_scan(qkv, feat, s0, *, tb):
    bsz, n_tok, _ = qkv.shape
    tb = min(tb, n_tok)
    n_blk = n_tok // tb
    n_chunks = tb // CHUNK
    gct = feat[..., GDN_HEADS:2 * GDN_HEADS].reshape(2, bsz, n_tok // CHUNK, CHUNK, GDN_HEADS).swapaxes(3, 4)

    def blk(d, i):
        return jnp.where(d == 0, i, n_blk - 1 - i)

    state_spec = pl.BlockSpec((1, 1, GDN_HEADS, GDN_DK, GDN_DV), lambda b, d, i: (b, d, 0, 0, 0))
    return pl.pallas_call(
        functools.partial(_gdn_scan_kernel, n_chunks=n_chunks),
        grid=(bsz, 2, n_blk),
        in_specs=[
            pl.BlockSpec((1, tb, GDN_KW), lambda b, d, i: (b, blk(d, i), 0)),
            pl.BlockSpec((1, tb, GDN_KW), lambda b, d, i: (b, blk(d, i), 1)),
            pl.BlockSpec((1, tb, GDN_VW), lambda b, d, i: (b, blk(d, i), 2)),
            pl.BlockSpec((1, 1, tb, LANES), lambda b, d, i: (d, b, blk(d, i), 0)),
            pl.BlockSpec((1, 1, n_chunks, GDN_HEADS, CHUNK), lambda b, d, i: (d, b, blk(d, i), 0, 0)),
            state_spec,
        ],
        out_specs=[
            pl.BlockSpec((1, 1, tb, GDN_VW), lambda b, d, i: (d, b, blk(d, i), 0)),
            state_spec,
        ],
        out_shape=[
            jax.ShapeDtypeStruct((2, bsz, n_tok, GDN_VW), F32),
            jax.ShapeDtypeStruct((bsz, 2, GDN_HEADS, GDN_DK, GDN_DV), F32),
        ],
        scratch_shapes=[pltpu.VMEM((GDN_HEADS, GDN_DK, GDN_DV), F32)],
        compiler_params=_cparams("parallel", "parallel", "arbitrary"),
        name="gdn_scan",
    )(qkv, qkv, qkv, feat, gct, s0)


def _gated_norm_kernel(of_ref, ob_ref, z_ref, w_ref, y_ref, *, head_dim):
    width = z_ref.shape[2]
    for h in range(width // head_dim):
        hs = slice(h * head_dim, (h + 1) * head_dim)
        o = of_ref[0, 0, :, hs] + ob_ref[0, 0, :, hs]
        z = z_ref[0, :, hs].astype(F32)
        y = o * lax.rsqrt(jnp.mean(o * o, axis=-1, keepdims=True) + EPS) * w_ref[...]
        y_ref[0, :, hs] = (y * (z * jax.nn.sigmoid(z))).astype(y_ref.dtype)


def gated_head_norm(o2, proj, z_name, w, *, head_dim, tm=512):
    _, bsz, n_tok, width = o2.shape
    tm = min(tm, n_tok)
    z_blk = WIDE_OFFSET[z_name][0] // width
    return pl.pallas_call(
        functools.partial(_gated_norm_kernel, head_dim=head_dim),
        grid=(bsz, n_tok // tm),
        in_specs=[
            pl.BlockSpec((1, 1, tm, width), lambda b, m: (0, b, m, 0)),
            pl.BlockSpec((1, 1, tm, width), lambda b, m: (1, b, m, 0)),
            pl.BlockSpec((1, tm, width), lambda b, m: (b, m, z_blk)),
            pl.BlockSpec((1, head_dim), lambda b, m: (0, 0)),
        ],
        out_specs=pl.BlockSpec((1, tm, width), lambda b, m: (b, m, 0)),
        out_shape=jax.ShapeDtypeStruct((bsz, n_tok, width), BF16),
        compiler_params=_cparams("parallel", "parallel"),
        name="gated_head_norm",
    )(o2, o2, proj, w.reshape(1, head_dim))


def gdn_branch(proj_l, narrow_l, proj_c, narrow_c, conv_w, a_log, dt_bias, norm_w):
    bsz = proj_l.shape[0]
    conv_w8 = jnp.pad(conv_w.astype(F32), ((0, 8 - CONV_W), (0, 0)))
    s0 = jnp.zeros((bsz, 2, GDN_HEADS, GDN_DK, GDN_DV), F32)
    o_c, s_c = gdn_scan(gdn_conv_act(proj_c, conv_w8, tm=512), gdn_features(narrow_c, a_log, dt_bias), s0, tb=SCAN_TB)
    o_l, _ = gdn_scan(gdn_conv_act(proj_l, conv_w8, tm=512), gdn_features(narrow_l, a_log, dt_bias), s_c, tb=SCAN_TB)
    y = gated_head_norm(o_l, proj_l, 'gdn_z', norm_w, head_dim=GDN_DV)
    yc = gated_head_norm(o_c, proj_c, 'gdn_z', norm_w, head_dim=GDN_DV)
    return y, yc


def _gla_scan_kernel(q_ref, k_ref, v_ref, lr_ref, w2_ref, b2_ref, cos_ref, sin_ref, s0_ref, o_ref, sfin_ref, s_scr, *,
                     n_chunks):
    d = pl.program_id(1)
    i = pl.program_id(2)

    @pl.when(i == 0)
    def _():
        s_scr[...] = s0_ref[0, 0]

    incl, _ = _scan_masks(d)
    tri = incl.astype(F32)
    lane = lax.broadcasted_iota(jnp.int32, (CHUNK, GLA_DK), 1)
    first_half = (lane % (GLA_DK // 2)) < (GLA_DK // 4)
    dot = functools.partial(jnp.dot, preferred_element_type=F32)

    def body(ci, carry):
        cc = jnp.where(d == 0, ci, n_chunks - 1 - ci)
        rows = pl.ds(pl.multiple_of(cc * CHUNK, CHUNK), CHUNK)
        pre = dot(lr_ref[0, rows, :].astype(BF16), w2_ref[0].astype(BF16)) + b2_ref[0]
        log_a = (jnp.minimum(pre, 0.0) - jnp.log(1.0 + jnp.exp(-jnp.abs(pre)))) * (1.0 / GLA_GATE_NORM)
        cum = dot(tri, log_a, precision=HIGHEST)
        total = jnp.sum(log_a, axis=0, keepdims=True)
        cos, sin = cos_ref[rows, :], sin_ref[rows, :]

        def rope(t):
            partner = jnp.where(first_half, pltpu.roll(t, GLA_DK - GLA_DK // 4, 1), pltpu.roll(t, GLA_DK // 4, 1))
            return t * cos + partner * sin

        for h in range(GLA_HEADS):
            ks = slice(h * GLA_DK, (h + 1) * GLA_DK)
            vs = slice(h * GLA_DV, (h + 1) * GLA_DV)
            b, bt = cum[:, ks], total[:, ks]
            qh = rope(q_ref[0, rows, ks].astype(F32)) * GLA_DK ** -0.5
            kh = rope(k_ref[0, rows, ks].astype(F32))
            vh = v_ref[0, rows, vs]
            q_dec = (qh * jnp.exp(b)).astype(BF16)
            k_inv = (kh * jnp.exp(-b)).astype(BF16)
            k_dec = (kh * jnp.exp(bt - b)).astype(BF16)
            scores = jnp.where(incl, lax.dot_general(q_dec, k_inv, NT_DIMS, preferred_element_type=F32), 0.0)
            st = s_scr[h]
            o_ref[0, 0, rows, vs] = dot(scores.astype(BF16), vh) + lax.dot_general(
                q_dec, st.astype(BF16), NT_DIMS, preferred_element_type=F32)
            s_scr[h] = st * jnp.exp(bt) + lax.dot_general(vh, k_dec, TN_DIMS, preferred_element_type=F32)
        return carry

    lax.fori_loop(0, n_chunks, body, 0)

    @pl.when(i == pl.num_programs(2) - 1)
    def _():
        sfin_ref[0, 0] = s_scr[...]


def gla_scan(proj, narrow, w2p, b2p, cos, sin, s0, *, tb):
    bsz, n_tok, _ = proj.shape
    tb = min(tb, n_tok)
    n_blk = n_tok // tb
    q_blk = WIDE_OFFSET['gla_q'][0] // GLA_KW
    k_blk = WIDE_OFFSET['gla_k'][0] // GLA_KW
    v_blk = WIDE_OFFSET['gla_v'][0] // GLA_VW

    def blk(d, i):
        return jnp.where(d == 0, i, n_blk - 1 - i)

    state_spec = pl.BlockSpec((1, 1, GLA_HEADS, GLA_DV, GLA_DK), lambda b, d, i: (b, d, 0, 0, 0))
    return pl.pallas_call(
        functools.partial(_gla_scan_kernel, n_chunks=tb // CHUNK),
        grid=(bsz, 2, n_blk),
        in_specs=[
            pl.BlockSpec((1, tb, GLA_KW), lambda b, d, i: (b, blk(d, i), q_blk)),
            pl.BlockSpec((1, tb, GLA_KW), lambda b, d, i: (b, blk(d, i), k_blk)),
            pl.BlockSpec((1, tb, GLA_VW), lambda b, d, i: (b, blk(d, i), v_blk)),
            pl.BlockSpec((1, tb, LANES), lambda b, d, i: (b, blk(d, i), 0)),
            pl.BlockSpec((1, LANES, GLA_KW), lambda b, d, i: (d, 0, 0)),
            pl.BlockSpec((1, 1, GLA_KW), lambda b, d, i: (d, 0, 0)),
            pl.BlockSpec((tb, GLA_DK), lambda b, d, i: (blk(d, i), 0)),
            pl.BlockSpec((tb, GLA_DK), lambda b, d, i: (blk(d, i), 0)),
            state_spec,
        ],
        out_specs=[
            pl.BlockSpec((1, 1, tb, GLA_VW), lambda b, d, i: (d, b, blk(d, i), 0)),
            state_spec,
        ],
        out_shape=[
            jax.ShapeDtypeStruct((2, bsz, n_tok, GLA_VW), F32),
            jax.ShapeDtypeStruct((bsz, 2, GLA_HEADS, GLA_DV, GLA_DK), F32),
        ],
        scratch_shapes=[pltpu.VMEM((GLA_HEADS, GLA_DV, GLA_DK), F32)],
        compiler_params=_cparams("parallel", "parallel", "arbitrary"),
        name="gla_scan",
    )(proj, proj, proj, narrow, w2p, b2p, cos, sin, s0)


def rope_tables(n_tokens):
    axis_dim = GLA_DK // 2
    inv_freq = ROPE_BASE ** (-jnp.arange(0, axis_dim, 2, dtype=F32) / axis_dim)
    pos = jnp.arange(n_tokens)
    ang_row = (pos // GRID_W).astype(F32)[:, None] * inv_freq
    ang_col = (pos % GRID_W).astype(F32)[:, None] * inv_freq
    cos = jnp.concatenate([jnp.cos(ang_row)] * 2 + [jnp.cos(ang_col)] * 2, axis=-1)
    sin = jnp.concatenate([-jnp.sin(ang_row), jnp.sin(ang_row), -jnp.sin(ang_col), jnp.sin(ang_col)], axis=-1)
    return cos, sin


def gla_branch(proj_l, narrow_l, proj_c, narrow_c, w2, b2, norm_w, rope_l):
    bsz = proj_l.shape[0]
    n_ctx = proj_c.shape[1]
    lr0 = NARROW_OFFSET['gla_lr'][0]
    w2p = jnp.zeros((2, LANES, GLA_KW), F32)
    for d in range(2):
        w2p = w2p.at[d, lr0 + GLA_LOWRANK * d:lr0 + GLA_LOWRANK * (d + 1), :].set(w2[d].astype(F32))
    b2p = b2.astype(F32).reshape(2, 1, GLA_KW)
    no_rope = (jnp.ones((n_ctx, GLA_DK), F32), jnp.zeros((n_ctx, GLA_DK), F32))
    s0 = jnp.zeros((bsz, 2, GLA_HEADS, GLA_DV, GLA_DK), F32)
    o_c, s_c = gla_scan(proj_c, narrow_c, w2p, b2p, *no_rope, s0, tb=SCAN_TB)
    o_l, _ = gla_scan(proj_l, narrow_l, w2p, b2p, *rope_l, s_c, tb=SCAN_TB)
    y = gated_head_norm(o_l, proj_l, 'gla_z', norm_w, head_dim=GLA_DV)
    yc = gated_head_norm(o_c, proj_c, 'gla_z', norm_w, head_dim=GLA_DV)
    return y, yc


def split_heads(t, n_heads):
    return t.reshape(t.shape[:-1] + (n_heads, t.shape[-1] // n_heads))


def merge_heads(t):
    return t.reshape(t.shape[:-2] + (t.shape[-2] * t.shape[-1],))


def rev(t):
    return None if t is None else jnp.flip(t, axis=1)


def l2_normalize(t):
    return t * lax.rsqrt(jnp.sum(t * t, axis=-1, keepdims=True) + EPS)


def centred_conv(t, w):
    pad = CONV_W // 2
    return lax.conv_general_dilated(
        t, w[:, None, :].astype(t.dtype), window_strides=(1,), padding=((pad, pad),),
        dimension_numbers=('NWC', 'WIO', 'NWC'), feature_group_count=t.shape[-1])


def axial_rope_angles(n_tokens, head_dim):
    axis_dim = head_dim // 2
    inv_freq = ROPE_BASE ** (-jnp.arange(0, axis_dim, 2, dtype=F32) / axis_dim)
    pos = jnp.arange(n_tokens)
    row = (pos // GRID_W).astype(F32)
    col = (pos % GRID_W).astype(F32)
    return row[:, None] * inv_freq, col[:, None] * inv_freq


def rope_rotate(t, ang):
    cos = jnp.cos(ang)[:, None, :]
    sin = jnp.sin(ang)[:, None, :]
    t1, t2 = jnp.split(t, 2, axis=-1)
    return jnp.concatenate([t1 * cos - t2 * sin, t2 * cos + t1 * sin], axis=-1)


def apply_rope2d(t, ang):
    ang_row, ang_col = ang
    tf = t.astype(F32)
    half = tf.shape[-1] // 2
    return jnp.concatenate([rope_rotate(tf[..., :half], ang_row), rope_rotate(tf[..., half:], ang_col)], axis=-1)


def gated_head_norm_ref(o, z, w):
    y = o * lax.rsqrt(jnp.mean(o * o, axis=-1, keepdims=True) + EPS) * w.astype(F32)
    return merge_heads(y) * jax.nn.silu(z.astype(F32))


def bidirectional_with_prefix(scan_fn, lat_fwd, lat_bwd, ctx_fwd, ctx_bwd, s0):
    oc_f, sc_f = scan_fn(*ctx_fwd, s0)
    ol_f, _ = scan_fn(*lat_fwd, sc_f)
    oc_b, sc_b = scan_fn(*[rev(t) for t in ctx_bwd], s0)
    ol_b, _ = scan_fn(*[rev(t) for t in lat_bwd], sc_b)
    o_lat = ol_f + rev(ol_b)
    o_ctx = None if oc_f is None else oc_f + rev(oc_b)
    return o_lat, o_ctx


def chunk_gated_delta(q, k, v, g, beta, s0):
    bsz, n_tok, n_heads, _ = k.shape
    n = n_tok // CHUNK
    to_chunks = lambda t: t.reshape(bsz, n, CHUNK, n_heads, t.shape[-1]).transpose(1, 0, 3, 2, 4)
    kc, vc = to_chunks(k), to_chunks(v)
    gc = jnp.cumsum(g.reshape(bsz, n, CHUNK, n_heads).transpose(1, 0, 3, 2), axis=-1)
    bc = beta.reshape(bsz, n, CHUNK, n_heads).transpose(1, 0, 3, 2)
    causal = jnp.tril(jnp.ones((CHUNK, CHUNK), bool))
    strict = jnp.tril(jnp.ones((CHUNK, CHUNK), bool), -1)
    decay = jnp.exp(jnp.where(causal, gc[..., :, None] - gc[..., None, :], -jnp.inf))
    kb = kc * bc[..., None]
    lower = jnp.where(strict, jnp.einsum('nbhik,nbhjk->nbhij', kb, kc) * decay, 0.0)
    unit_lower = lower + jnp.eye(CHUNK, dtype=lower.dtype)
    u = lax.linalg.triangular_solve(unit_lower, vc * bc[..., None], left_side=True, lower=True, unit_diagonal=True)
    w = lax.linalg.triangular_solve(unit_lower, kb * jnp.exp(gc)[..., None], left_side=True, lower=True,
                                    unit_diagonal=True)
    g_last = gc[..., -1]
    k_dec = kc * jnp.exp(g_last[..., None] - gc)[..., None]

    def new_values(s, u_i, w_i):
        return u_i - jnp.einsum('bhck,bhkv->bhcv', w_i, s)

    def update(s, v_new, kd_i, gl_i):
        return s * jnp.exp(gl_i)[..., None, None] + jnp.einsum('bhck,bhcv->bhkv', kd_i, v_new)

    if q is None:
        def state_step(s, xs):
            u_i, w_i, kd_i, gl_i = xs
            return update(s, new_values(s, u_i, w_i), kd_i, gl_i), None
        s_fin, _ = lax.scan(state_step, s0, (u, w, k_dec, g_last))
        return None, s_fin

    qc = to_chunks(q)
    q_dec = qc * jnp.exp(gc)[..., None]
    qk = jnp.where(causal, jnp.einsum('nbhik,nbhjk->nbhij', qc, kc) * decay, 0.0)

    def step(s, xs):
        u_i, w_i, kd_i, gl_i, qd_i, qk_i = xs
        v_new = new_values(s, u_i, w_i)
        o = jnp.einsum('bhck,bhkv->bhcv', qd_i, s) + jnp.einsum('bhij,bhjv->bhiv', qk_i, v_new)
        return update(s, v_new, kd_i, gl_i), o

    s_fin, o = lax.scan(step, s0, (u, w, k_dec, g_last, q_dec, qk))
    return o.transpose(1, 0, 3, 2, 4).reshape(bsz, n_tok, n_heads, -1), s_fin


def chunk_gla(q, k, v, log_a, s0):
    bsz, n_tok, n_heads, _ = k.shape
    n = n_tok // CHUNK
    to_chunks = lambda t: t.reshape(bsz, n, CHUNK, n_heads, t.shape[-1]).transpose(1, 0, 3, 2, 4)
    kc, vc = to_chunks(k), to_chunks(v)
    b = jnp.cumsum(to_chunks(log_a), axis=-2)
    b_last = b[..., -1:, :]
    k_dec = kc * jnp.exp(b_last - b)
    d_last = jnp.exp(b_last[..., 0, :])

    def update(s, k_i, v_i, d_i):
        return d_i[..., None] * s + jnp.einsum('bhck,bhcv->bhkv', k_i, v_i)

    if q is None:
        s_fin, _ = lax.scan(lambda s, xs: (update(s, *xs), None), s0, (k_dec, vc, d_last))
        return None, s_fin

    q_dec = to_chunks(q) * jnp.exp(b)
    k_inv = kc * jnp.exp(-b)
    causal = jnp.tril(jnp.ones((CHUNK, CHUNK), bool))
    scores = jnp.where(causal, jnp.einsum('nbhik,nbhjk->nbhij', q_dec, k_inv), 0.0)
    o_intra = jnp.einsum('nbhij,nbhjv->nbhiv', scores, vc)

    def step(s, xs):
        q_i, k_i, v_i, d_i = xs
        return update(s, k_i, v_i, d_i), jnp.einsum('bhck,bhkv->bhcv', q_i, s)

    s_fin, o_inter = lax.scan(step, s0, (q_dec, k_dec, vc, d_last))
    o = (o_intra + o_inter).transpose(1, 0, 3, 2, 4).reshape(bsz, n_tok, n_heads, -1)
    return o, s_fin


def gdn_prepare(pg, conv_w, a_log, dt_bias):
    bsz, n_tok, _ = pg['gdn_k'].shape
    conv_q, conv_k, conv_v = conv_w[:, :GDN_KW], conv_w[:, GDN_KW:2 * GDN_KW], conv_w[:, 2 * GDN_KW:]

    def conv_act(t, w):
        return split_heads(jax.nn.silu(centred_conv(t.astype(F32), w)), GDN_HEADS)

    k = l2_normalize(conv_act(pg['gdn_k'], conv_k))
    v = conv_act(pg['gdn_v'], conv_v)
    q = l2_normalize(conv_act(pg['gdn_q'], conv_q)) * GDN_DK ** -0.5 if 'gdn_q' in pg else None
    beta = jax.nn.sigmoid(pg['gdn_beta'].astype(F32)).reshape(bsz, n_tok, 2, GDN_HEADS)
    g = -jnp.exp(a_log.astype(F32)) * jax.nn.softplus(
        pg['gdn_a'].astype(F32).reshape(bsz, n_tok, 2, GDN_HEADS) + dt_bias.astype(F32))
    return q, k, v, beta, g


def gdn_mixer(pl_, pc, conv_w, a_log, dt_bias, norm_w, ctx_out):
    ql, kl, vl, bl, gl = gdn_prepare(pl_, conv_w, a_log, dt_bias)
    qc, kc, vc, bc, gc = gdn_prepare(pc, conv_w, a_log, dt_bias)
    s0 = jnp.zeros((kl.shape[0], GDN_HEADS, GDN_DK, GDN_DV), F32)
    o_lat, o_ctx = bidirectional_with_prefix(
        chunk_gated_delta,
        (ql, kl, vl, gl[:, :, 0], bl[:, :, 0]), (ql, kl, vl, gl[:, :, 1], bl[:, :, 1]),
        (qc, kc, vc, gc[:, :, 0], bc[:, :, 0]), (qc, kc, vc, gc[:, :, 1], bc[:, :, 1]), s0)
    y = gated_head_norm_ref(o_lat, pl_['gdn_z'], norm_w)
    yc = gated_head_norm_ref(o_ctx, pc['gdn_z'], norm_w) if ctx_out else None
    return y, yc


def gla_prepare(pg, w2, b2, ang):
    bsz, n_tok, _ = pg['gla_k'].shape

    def qk_heads(t):
        t = split_heads(t.astype(F32), GLA_HEADS)
        if ang is not None:
            t = apply_rope2d(t, ang)
        return t

    k = qk_heads(pg['gla_k'])
    q = qk_heads(pg['gla_q']) * GLA_DK ** -0.5 if 'gla_q' in pg else None
    v = split_heads(pg['gla_v'], GLA_HEADS).astype(F32)
    lr = pg['gla_lr'].astype(F32).reshape(bsz, n_tok, 2, GLA_LOWRANK)
    log_a = jax.nn.log_sigmoid(jnp.einsum('btzr,zrk->btzk', lr, w2.astype(F32)) + b2.astype(F32)) / GLA_GATE_NORM
    return q, k, v, log_a.reshape(bsz, n_tok, 2, GLA_HEADS, GLA_DK)


def gla_mixer(pl_, pc, w2, b2, norm_w, ang, ctx_out):
    ql, kl, vl, al = gla_prepare(pl_, w2, b2, ang)
    qc, kc, vc, ac = gla_prepare(pc, w2, b2, None)
    s0 = jnp.zeros((kl.shape[0], GLA_HEADS, GLA_DK, GLA_DV), F32)
    o_lat, o_ctx = bidirectional_with_prefix(
        chunk_gla,
        (ql, kl, vl, al[:, :, 0]), (ql, kl, vl, al[:, :, 1]),
        (qc, kc, vc, ac[:, :, 0]), (qc, kc, vc, ac[:, :, 1]), s0)
    y = gated_head_norm_ref(o_lat, pl_['gla_z'], norm_w)
    yc = gated_head_norm_ref(o_ctx, pc['gla_z'], norm_w) if ctx_out else None
    return y, yc


def split_groups(proj, narrow, drop=()):
    out = {}
    for name in WIDE_GROUPS:
        if name not in drop:
            off, width = WIDE_OFFSET[name]
            out[name] = proj[..., off:off + width]
    for name in NARROW_GROUPS:
        off, width = NARROW_OFFSET[name]
        out[name] = narrow[..., off:off + width]
    return out


def expert_choice_ffn(h, logits, w_gate, w_up, w_down):
    bsz, n_tok, d = h.shape
    cap = max(1, CAPACITY_FACTOR * n_tok // N_EXPERTS)
    affinity = jax.nn.softmax(logits, axis=-1)
    gate, idx = lax.top_k(jnp.swapaxes(affinity, 1, 2), cap)
    xs = jax.vmap(lambda hb, ib: hb[ib])(h, idx)
    ys = expert_ffn(xs, gate[..., None], w_gate, w_up, w_down, tc=512)
    return jax.vmap(lambda yb, ib: jnp.zeros((n_tok, d), F32).at[ib.reshape(-1)].add(yb.reshape(-1, d)))(ys, idx)


def pack_in_weight(w_in):
    wide = jnp.concatenate([w_in[:, IN_OFFSET[n][0]:IN_OFFSET[n][0] + IN_OFFSET[n][1]] for n in WIDE_GROUPS], axis=1)
    narrow = jnp.concatenate([w_in[:, IN_OFFSET[n][0]:IN_OFFSET[n][0] + IN_OFFSET[n][1]] for n in NARROW_GROUPS],
                             axis=1)
    narrow = jnp.pad(narrow, ((0, 0), (0, NARROW_WIDTH - narrow.shape[1])))
    return wide.astype(BF16), narrow.astype(BF16)


def kernel(x, c, ctx, c_ctx, norm1_w, norm2_w, final_norm_w, w_ada, b_ada, w_in, na_rpb, gdn_conv,
           gdn_a_log, gdn_dt_bias, gdn_norm_w, gla_w2, gla_b2, gla_norm_w, w_branch, w_out,
           w_router, w_gate, w_up, w_down):
    bsz, n_lat, d = x.shape
    rows = n_lat // GRID_W
    ang = axial_rope_angles(n_lat, GLA_DK)
    cond_lat = jax.nn.silu(c)
    cond_ctx = jnp.broadcast_to(jax.nn.silu(c_ctx)[None], (bsz, d))
    for layer in range(DEPTH):
        ctx_out = layer < DEPTH - 1
        mod_lat = jnp.split((cond_lat @ w_ada[layer] + b_ada[layer])[:, None, :], 6, axis=-1)
        mod_ctx = jnp.split((cond_ctx @ w_ada[layer] + b_ada[layer])[:, None, :], 6, axis=-1)
        w_wide, w_narrow = pack_in_weight(w_in[layer])
        wb = w_branch[layer].astype(BF16)
        wo = w_out[layer].astype(BF16)
        wg, wu, wd = w_gate[layer].astype(BF16), w_up[layer].astype(BF16), w_down[layer].astype(BF16)
        wr = jnp.pad(w_router[layer], ((0, 0), (0, LANES - N_EXPERTS)))

        proj_l, narrow_l = norm_mod_project(x, norm1_w[layer], mod_lat[0], mod_lat[1], w_wide, w_narrow,
                                            tm=1024, tn=1024)
        proj_c, narrow_c = norm_mod_project(ctx, norm1_w[layer], mod_ctx[0], mod_ctx[1], w_wide, w_narrow,
                                            tm=256, tn=1024)
        ya = neighbourhood_attention(proj_l, proj_c, na_rpb[layer], rows)
        yca = context_attention(proj_c) if ctx_out else None
        pl_ = split_groups(proj_l, narrow_l)
        pc = split_groups(proj_c, narrow_c)
        if not ctx_out:
            pc.pop('gdn_q')
            pc.pop('gla_q')
        yb, ycb = gdn_mixer(pl_, pc, gdn_conv[layer], gdn_a_log[layer], gdn_dt_bias[layer], gdn_norm_w[layer],
                            ctx_out)
        yb = yb.astype(BF16)
        yg, ycg = gla_mixer(pl_, pc, gla_w2[layer], gla_b2[layer], gla_norm_w[layer], ang, ctx_out)
        yg = yg.astype(BF16)

        y = merge_branches(proj_l, ya, yb, yg, wb, tm=1024, tn=1024)
        x = out_proj_residual(y, wo, x, mod_lat[2], tm=1024, tn=1024)
        h2, logits = norm_mod_router(x, norm2_w[layer], mod_lat[3], mod_lat[4], wr, tm=512)
        x = x + mod_lat[5] * expert_choice_ffn(h2, logits[..., :N_EXPERTS], wg, wu, wd)
        if ctx_out:
            yc = merge_branches(proj_c, yca.astype(BF16), ycb.astype(BF16), ycg.astype(BF16), wb, tm=256, tn=1024)
            ctx = out_proj_residual(yc, wo, ctx, mod_ctx[2], tm=256, tn=1024)
            hc2, logits_c = norm_mod_router(ctx, norm2_w[layer], mod_ctx[3], mod_ctx[4], wr, tm=256)
            ctx = ctx + mod_ctx[5] * expert_choice_ffn(hc2, logits_c[..., :N_EXPERTS], wg, wu, wd)
    return final_norm(x, final_norm_w, tm=512)
```
